```python
import functools
import jax, jax.numpy as jnp
from jax import lax
import numpy as np


D_MODEL = 1024
BATCH = 4
SEQ = 4096
DEPTH = 4
DEC_BATCH = 128
DEC_SEQ = 1
PAST_LEN = 2048
PAGE_SIZE = 128

D_CONV = 256
CONV_W = 31
D_POOL = 256
POOL_WINDOWS = (2, 4, 8, 16)
N_POOL_GROUPS = 4
POOL_GW = D_POOL // N_POOL_GROUPS
N_HEADS = 8
HEAD_DIM = 64
D_ATTN = N_HEADS * HEAD_DIM
D_MIX = D_CONV + D_POOL + D_ATTN
N_IDX_HEADS = 8
D_IDX = 64
TOPK_MAX = 256
Q_BLOCK = 128
D_FF = 2816
ALPHA = (2 * DEPTH) ** 0.25
BETA = (8 * DEPTH) ** -0.25
LN_EPS = 1e-5
CONV_STATE = CONV_W - 1
POOL_STATE = max(POOL_WINDOWS) - 1
_WIDTHS = (2 * D_CONV, D_POOL, D_ATTN, D_ATTN, D_ATTN, N_IDX_HEADS * D_IDX, D_IDX, N_IDX_HEADS)
D_IN = sum(_WIDTHS)
_SPLITS = tuple(int(v) for v in np.cumsum(_WIDTHS)[:-1])

kernel_name = "hybrid_conv_pool_dsa_deepnorm_step"


def layer_norm(x, g, b):
    xf = x.astype(jnp.float32)
    mu = jnp.mean(xf, -1, keepdims=True)
    var = jnp.mean(jnp.square(xf - mu), -1, keepdims=True)
    return ((xf - mu) * lax.rsqrt(var + LN_EPS) * g.astype(jnp.float32) + b.astype(jnp.float32)).astype(x.dtype)


def swiglu(h, w_in, w_out):
    a, g = jnp.split(h @ w_in, 2, axis=-1)
    return (jax.nn.silu(g) * a) @ w_out


def alibi_slopes():
    return jnp.exp2(-8.0 * jnp.arange(1, N_HEADS + 1, dtype=jnp.float32) / N_HEADS)


def conv_mixer(u, buf, w_dw, b_dw, g_cln, b_cln):
    a, gt = jnp.split(u, 2, axis=-1)
    v = a * jax.nn.sigmoid(gt)
    v_ext = jnp.concatenate([buf, v], axis=1)
    y = lax.conv_general_dilated(v_ext, w_dw[:, None, :], window_strides=(1,), padding='VALID',
                                 dimension_numbers=('NWC', 'WIO', 'NWC'),
                                 feature_group_count=D_CONV) + b_dw
    y = jax.nn.silu(layer_norm(y, g_cln, b_cln))
    return y, v_ext[:, -CONV_STATE:]


def pool_mixer(u, buf, pos0, w_pool, s_pool):
    T = u.shape[1]
    u_ext = jnp.concatenate([buf, u], axis=1)
    cs = jnp.cumsum(u_ext.astype(jnp.float32), axis=1)
    cs = jnp.concatenate([jnp.zeros_like(cs[:, :1]), cs], axis=1)
    pos = pos0 + jnp.arange(T)
    outs = []
    for gi, w in enumerate(POOL_WINDOWS):
        sl = slice(gi * POOL_GW, (gi + 1) * POOL_GW)
        win = cs[:, POOL_STATE + 1:POOL_STATE + 1 + T, sl] - cs[:, POOL_STATE + 1 - w:POOL_STATE + 1 - w + T, sl]
        cnt = jnp.minimum(pos + 1, w).astype(jnp.float32)[None, :, None]
        d = (win / cnt - u[..., sl].astype(jnp.float32)).astype(u.dtype)
        outs.append(d @ w_pool[gi])
    y = jnp.concatenate(outs, axis=-1) * s_pool
    return y, u_ext[:, -POOL_STATE:]


def index_select(qi, wi, ki, q_pos, topk):
    L = ki.shape[1]
    s = jnp.einsum('bthd,bsd->bths', qi, ki).astype(jnp.float32) * (D_IDX ** -0.5)
    score = jnp.einsum('bths,bth->bts', jax.nn.relu(s), wi.astype(jnp.float32)) * (N_IDX_HEADS ** -0.5)
    causal = jnp.arange(L)[None, None, :] <= q_pos[None, :, None]
    score = jnp.where(causal, score, -jnp.inf)
    _, idx = lax.top_k(score, topk)
    valid = idx <= q_pos[None, :, None]
    return idx, valid


def sparse_attend(q, kg, vg, sel_pos, q_pos, valid):
    logits = jnp.einsum('bthd,btkhd->bthk', q, kg).astype(jnp.float32) * (HEAD_DIM ** -0.5)
    dist = (q_pos[None, :, None] - sel_pos).astype(jnp.float32)
    logits = logits - alibi_slopes()[None, None, :, None] * dist[:, :, None, :]
    logits = jnp.where(valid[:, :, None, :], logits, -jnp.inf)
    p = jax.nn.softmax(logits, axis=-1).astype(vg.dtype)
    return jnp.einsum('bthk,btkhd->bthd', p, vg)


def _take_rows(a, idx):
    return jax.vmap(lambda ab, ib: ab[ib])(a, idx)


def attn_prompt(q, k, v, qi, ki, wi):
    B, S = q.shape[0], q.shape[1]
    topk = min(TOPK_MAX, S // 4)

    def block(i):
        t0 = i * Q_BLOCK
        sl = lambda a: lax.dynamic_slice_in_dim(a, t0, Q_BLOCK, axis=1)
        q_pos = t0 + jnp.arange(Q_BLOCK)
        idx, valid = index_select(sl(qi), sl(wi), ki, q_pos, topk)
        return sparse_attend(sl(q), _take_rows(k, idx), _take_rows(v, idx), idx, q_pos, valid)

    out = lax.map(block, jnp.arange(S // Q_BLOCK))
    return jnp.transpose(out, (1, 0, 2, 3, 4)).reshape(B, S, D_ATTN)


def attn_sample(q, k, v, qi, ki, wi, ck, cv, cki, page_table):
    Bd, T = q.shape[0], q.shape[1]
    past = page_table.shape[1] * PAGE_SIZE
    topk = min(TOPK_MAX, (past + T) // 4)
    ki_all = jnp.concatenate([cki[page_table].reshape(Bd, past, D_IDX), ki], axis=1)
    q_pos = past + jnp.arange(T)
    idx, valid = index_select(qi, wi, ki_all, q_pos, topk)
    is_past = idx < past
    pidx = jnp.minimum(idx, past - 1)
    phys = jax.vmap(lambda pt, ib: pt[ib])(page_table, pidx // PAGE_SIZE)
    off = pidx % PAGE_SIZE
    nidx = jnp.clip(idx - past, 0, T - 1)
    sel = is_past[..., None, None]
    kg = jnp.where(sel, ck[phys, off], _take_rows(k, nidx))
    vg = jnp.where(sel, cv[phys, off], _take_rows(v, nidx))
    return sparse_attend(q, kg, vg, idx, q_pos, valid).reshape(Bd, T, D_ATTN)


def trunk_layer(x, c, attn_fn, conv_buf, pool_buf, pos0,
                w_ada, b_ada, ln_g, ln_b, w_ff_in, w_ff_out, w_in, w_dw, b_dw,
                g_cln, b_cln, w_pool, s_pool, w_out):
    B, T = x.shape[0], x.shape[1]
    mod = (jax.nn.silu(c) @ w_ada + b_ada)[:, None, :]
    sh1, sc1, g1, sh2, sc2, g2, sh3, sc3, g3 = jnp.split(mod, 9, axis=-1)
    h = x * (1 + sc1) + sh1
    x = layer_norm(ALPHA * x + 0.5 * g1 * swiglu(h, w_ff_in[0], w_ff_out[0]), ln_g[0], ln_b[0])
    h = x * (1 + sc2) + sh2
    u_conv, u_pool, q, k, v, qi, ki, wi = jnp.split(h @ w_in, _SPLITS, axis=-1)
    y_conv, conv_new = conv_mixer(u_conv, conv_buf, w_dw, b_dw, g_cln, b_cln)
    y_pool, pool_new = pool_mixer(u_pool, pool_buf, pos0, w_pool, s_pool)
    q = q.reshape(B, T, N_HEADS, HEAD_DIM)
    k = k.reshape(B, T, N_HEADS, HEAD_DIM)
    v = v.reshape(B, T, N_HEADS, HEAD_DIM)
    qi = qi.reshape(B, T, N_IDX_HEADS, D_IDX)
    y_attn = attn_fn(q, k, v, qi, ki, wi)
    y = jnp.concatenate([y_conv, y_pool, y_attn], axis=-1) @ w_out
    x = layer_norm(ALPHA * x + g2 * y, ln_g[1], ln_b[1])
    h = x * (1 + sc3) + sh3
    x = layer_norm(ALPHA * x + 0.5 * g3 * swiglu(h, w_ff_in[1], w_ff_out[1]), ln_g[2], ln_b[2])
    return x, (k, v, ki, conv_new, pool_new)


def setup_inputs(seed: int = 0) -> dict:
    key = jax.random.key(seed)
    ks = jax.random.split(key, 24)
    n_pages = PAST_LEN // PAGE_SIZE
    n_used = DEC_BATCH * n_pages
    n_phys = n_used + n_used // 4

    def nrm(k, shape, scale=1.0):
        return jax.random.normal(k, shape, jnp.float32) * scale

    page_table = jax.random.permutation(ks[0], n_phys)[:n_used].reshape(DEC_BATCH, n_pages).astype(jnp.int32)
    return {
        "x_prompt": nrm(ks[1], (BATCH, SEQ, D_MODEL)),
        "x_sample": nrm(ks[2], (DEC_BATCH, DEC_SEQ, D_MODEL)),
        "cache_k": nrm(ks[3], (DEPTH, n_phys, PAGE_SIZE, N_HEADS, HEAD_DIM)),
        "cache_v": nrm(ks[4], (DEPTH, n_phys, PAGE_SIZE, N_HEADS, HEAD_DIM)),
        "cache_kidx": nrm(ks[5], (DEPTH, n_phys, PAGE_SIZE, D_IDX)),
        "state_conv": nrm(ks[6], (DEPTH, DEC_BATCH, CONV_STATE, D_CONV), 0.5),
        "state_pool": nrm(ks[7], (DEPTH, DEC_BATCH, POOL_STATE, D_POOL)),
        "page_table": page_table,
        "c_prompt": nrm(ks[8], (BATCH, D_MODEL)),
        "c_sample": nrm(ks[9], (DEC_BATCH, D_MODEL)),
        "w_ada": nrm(ks[10], (DEPTH, D_MODEL, 9 * D_MODEL), 0.5 * D_MODEL ** -0.5),
        "b_ada": nrm(ks[11], (DEPTH, 9 * D_MODEL), 0.01),
        "ln_g": 1.0 + nrm(ks[12], (DEPTH, 3, D_MODEL), 0.01),
        "ln_b": nrm(ks[13], (DEPTH, 3, D_MODEL), 0.01),
        "w_ff_in": nrm(ks[14], (DEPTH, 2, D_MODEL, 2 * D_FF), D_MODEL ** -0.5),
        "w_ff_out": nrm(ks[15], (DEPTH, 2, D_FF, D_MODEL), BETA * D_FF ** -0.5),
        "w_in": nrm(ks[16], (DEPTH, D_MODEL, D_IN), D_MODEL ** -0.5),
        "w_dw": nrm(ks[17], (DEPTH, CONV_W, D_CONV), CONV_W ** -0.5),
        "b_dw": nrm(ks[18], (DEPTH, D_CONV), 0.01),
        "g_conv_ln": 1.0 + nrm(ks[19], (DEPTH, D_CONV), 0.01),
        "b_conv_ln": nrm(ks[20], (DEPTH, D_CONV), 0.01),
        "w_pool": nrm(ks[21], (DEPTH, N_POOL_GROUPS, POOL_GW, POOL_GW), POOL_GW ** -0.5),
        "s_pool": 1.0 + nrm(ks[22], (DEPTH, D_POOL), 0.01),
        "w_out": nrm(ks[23], (DEPTH, D_MIX, D_MODEL), BETA * D_MIX ** -0.5),
    }


def reference(x_prompt, x_sample, cache_k, cache_v, cache_kidx, state_conv, state_pool, page_table,
              c_prompt, c_sample, w_ada, b_ada, ln_g, ln_b, w_ff_in, w_ff_out, w_in, w_dw, b_dw,
              g_conv_ln, b_conv_ln, w_pool, s_pool, w_out):
    bp = x_prompt.shape[0]
    past = page_table.shape[1] * PAGE_SIZE
    zero_conv = jnp.zeros((bp, CONV_STATE, D_CONV), x_prompt.dtype)
    zero_pool = jnp.zeros((bp, POOL_STATE, D_POOL), x_prompt.dtype)
    xp, xs = x_prompt, x_sample
    st_p, st_s = [], []
    for l in range(DEPTH):
        lw = (w_ada[l], b_ada[l], ln_g[l], ln_b[l], w_ff_in[l], w_ff_out[l], w_in[l], w_dw[l], b_dw[l],
              g_conv_ln[l], b_conv_ln[l], w_pool[l], s_pool[l], w_out[l])
        xp, sp = trunk_layer(xp, c_prompt, attn_prompt, zero_conv, zero_pool, 0, *lw)
        attn_s = functools.partial(attn_sample, ck=cache_k[l], cv=cache_v[l], cki=cache_kidx[l],
                                   page_table=page_table)
        xs, ss = trunk_layer(xs, c_sample, attn_s, state_conv[l], state_pool[l], past, *lw)
        st_p.append(sp)
        st_s.append(ss)
    k_prompt = jnp.stack([s[0] for s in st_p])
    v_prompt = jnp.stack([s[1] for s in st_p])
    kidx_prompt = jnp.stack([s[2] for s in st_p])
    conv_prompt = jnp.stack([s[3] for s in st_p])
    pool_prompt = jnp.stack([s[4] for s in st_p])
    k_sample = jnp.stack([s[0] for s in st_s])
    v_sample = jnp.stack([s[1] for s in st_s])
    kidx_sample = jnp.stack([s[2] for s in st_s])
    conv_sample = jnp.stack([s[3] for s in st_s])
    pool_sample = jnp.stack([s[4] for s in st_s])
    return (xp, xs, k_prompt, v_prompt, kidx_prompt, conv_prompt, pool_prompt,
            k_sample, v_sample, kidx_sample, conv_sample, pool_sample)
```

```python
import functools

import jax
import jax.numpy as jnp
from jax import lax
from jax.experimental import pallas as pl
from jax.experimental.pallas import tpu as pltpu

F32 = jnp.float32
BF16 = jnp.bfloat16
I32 = jnp.int32

D_CONV = 256
CONV_W = 31
CONV_STATE = CONV_W - 1
D_POOL = 256
POOL_WINDOWS = (2, 4, 8, 16)
POOL_GW = D_POOL // len(POOL_WINDOWS)
POOL_STATE = max(POOL_WINDOWS) - 1
N_HEADS = 8
HEAD_DIM = 64
D_ATTN = N_HEADS * HEAD_DIM
N_IDX_HEADS = 8
D_IDX = 64
TOPK_MAX = 256
PAGE_SIZE = 128
LN_EPS = 1e-5
D_IN = 2 * D_CONV + D_POOL + 3 * D_ATTN + N_IDX_HEADS * D_IDX + D_IDX + N_IDX_HEADS

LANES = 128
D_IN_PAD = -(-D_IN // LANES) * LANES
COL_KIWI = 2 * D_CONV + D_POOL + 3 * D_ATTN + N_IDX_HEADS * D_IDX
HALO = 32
INT_MIN = -2 ** 31
INT_MAX = 2 ** 31 - 1
MASK_BIAS = -1e30
VMEM_LIMIT = 56 * 1024 * 1024

ALIBI_SLOPES = tuple(2.0 ** (-8.0 * (h + 1) / N_HEADS) for h in range(N_HEADS))
IDX_SCALE = (D_IDX ** -0.5) * (N_IDX_HEADS ** -0.5)


def _layer_norm(y, g, b):
    mu = jnp.mean(y, axis=-1, keepdims=True)
    d = y - mu
    var = jnp.mean(d * d, axis=-1, keepdims=True)
    return d * lax.rsqrt(var + LN_EPS) * g + b


def _silu(x):
    return x * jax.nn.sigmoid(x)


def _dot(a, b):
    return jnp.dot(a, b, preferred_element_type=F32)


def _dot_nt(a, b):
    return lax.dot_general(a, b, (((1,), (1,)), ((), ())), preferred_element_type=F32)


def _cparams(sem):
    return pltpu.CompilerParams(dimension_semantics=sem, vmem_limit_bytes=VMEM_LIMIT)


def _ada_kernel(c_ref, w_ref, b_ref, o_ref):
    s = _silu(c_ref[...]).astype(BF16)
    o_ref[...] = _dot(s, w_ref[...].astype(BF16)) + b_ref[...]


def _ada(c_all, w_ada, b_ada):
    depth, d, nd = w_ada.shape
    r = c_all.shape[0]
    return pl.pallas_call(
        _ada_kernel,
        grid=(depth, nd // d),
        in_specs=[
            pl.BlockSpec((r, d), lambda l, j: (0, 0)),
            pl.BlockSpec((None, d, d), lambda l, j: (l, 0, j)),
            pl.BlockSpec((None, 1, d), lambda l, j: (l, 0, j)),
        ],
        out_specs=pl.BlockSpec((None, r, d), lambda l, j: (l, 0, j)),
        out_shape=jax.ShapeDtypeStruct((depth, r, nd), F32),
        compiler_params=_cparams(("arbitrary", "arbitrary")),
        name="ada",
    )(c_all, w_ada, b_ada.reshape(depth, 1, nd))


def _ffn_kernel(x_ref, sh_ref, sc_ref, g_ref, wa_ref, wg_ref, wo_ref, lng_ref, lnb_ref, o_ref,
                acc_ref, h_ref, *, alpha):
    j = pl.program_id(1)

    @pl.when(j == 0)
    def _():
        h_ref[...] = (x_ref[...] * (1.0 + sc_ref[...]) + sh_ref[...]).astype(BF16)
        acc_ref[...] = jnp.zeros_like(acc_ref)

    h = h_ref[...]
    a = _dot(h, wa_ref[...])
    g = _dot(h, wg_ref[...])
    u = (_silu(g) * a).astype(BF16)
    acc_ref[...] += _dot(u, wo_ref[...])

    @pl.when(j == pl.num_programs(1) - 1)
    def _():
        y = alpha * x_ref[...] + 0.5 * g_ref[...] * acc_ref[...]
        o_ref[...] = _layer_norm(y, lng_ref[...], lnb_ref[...])


def _mod_spec(per_row, rows_per_seq, tm, d, col):
    if per_row:
        return pl.BlockSpec((None, tm, d), lambda i, *_: (0, i, col))
    return pl.BlockSpec((None, 1, d), lambda i, *_: ((i * tm) // rows_per_seq, 0, col))


def _ffn(x, mod, col0, w_in, w_out, lng, lnb, *, alpha, per_row, rows_per_seq, tm, tf):
    n, d = x.shape
    f = w_out.shape[0]
    nf = f // tf
    ms = functools.partial(_mod_spec, per_row, rows_per_seq, tm, d)
    return pl.pallas_call(
        functools.partial(_ffn_kernel, alpha=alpha),
        grid=(n // tm, nf),
        in_specs=[
            pl.BlockSpec((tm, d), lambda i, j: (i, 0)),
            ms(col0), ms(col0 + 1), ms(col0 + 2),
            pl.BlockSpec((d, tf), lambda i, j: (0, j)),
            pl.BlockSpec((d, tf), lambda i, j: (0, j + nf)),
            pl.BlockSpec((tf, d), lambda i, j: (j, 0)),
            pl.BlockSpec((1, d), lambda i, j: (0, 0)),
            pl.BlockSpec((1, d), lambda i, j: (0, 0)),
        ],
        out_specs=pl.BlockSpec((tm, d), lambda i, j: (i, 0)),
        out_shape=jax.ShapeDtypeStruct((n, d), F32),
        scratch_shapes=[pltpu.VMEM((tm, d), F32), pltpu.VMEM((tm, d), BF16)],
        compiler_params=_cparams(("arbitrary", "arbitrary")),
        name="ffn",
    )(x, mod, mod, mod, w_in, w_in, w_out, lng, lnb)


def _inproj_kernel(x_ref, sh_ref, sc_ref, w_ref, vglu_ref, upool_ref, q_ref, k_ref, v_ref, kb_ref, vb_ref,
                   qi_ref, kiwi_ref, kiwib_ref):
    h = (x_ref[...] * (1.0 + sc_ref[...]) + sh_ref[...]).astype(BF16)
    y = _dot(h, w_ref[...])
    c = 0
    a = y[:, c:c + D_CONV]
    gt = y[:, c + D_CONV:c + 2 * D_CONV]
    vglu_ref[...] = a * jax.nn.sigmoid(gt)
    c += 2 * D_CONV
    upool_ref[...] = y[:, c:c + D_POOL]
    c += D_POOL
    q_ref[...] = (y[:, c:c + D_ATTN] * (HEAD_DIM ** -0.5)).astype(BF16)
    c += D_ATTN
    k = y[:, c:c + D_ATTN]
    k_ref[...] = k
    kb_ref[...] = k.astype(BF16)
    c += D_ATTN
    v = y[:, c:c + D_ATTN]
    v_ref[...] = v
    vb_ref[...] = v.astype(BF16)
    c += D_ATTN
    qi_ref[...] = y[:, c:c + N_IDX_HEADS * D_IDX].astype(BF16)
    c += N_IDX_HEADS * D_IDX
    kiwi = y[:, c:c + LANES]
    kiwi_ref[...] = kiwi
    kiwib_ref[...] = kiwi.astype(BF16)


def _inproj(x, mod, w_in_pad, *, per_row, rows_per_seq, tm):
    n, d = x.shape
    ms = functools.partial(_mod_spec, per_row, rows_per_seq, tm, d)
    row = lambda w: pl.BlockSpec((tm, w), lambda i: (i, 0))
    shp = lambda w, dt: jax.ShapeDtypeStruct((n, w), dt)
    return pl.pallas_call(
        _inproj_kernel,
        grid=(n // tm,),
        in_specs=[row(d), ms(3), ms(4), pl.BlockSpec((d, D_IN_PAD), lambda i: (0, 0))],
        out_specs=[row(D_CONV), row(D_POOL), row(D_ATTN), row(D_ATTN), row(D_ATTN), row(D_ATTN), row(D_ATTN),
                   row(N_IDX_HEADS * D_IDX), row(LANES), row(LANES)],
        out_shape=[shp(D_CONV, F32), shp(D_POOL, F32), shp(D_ATTN, BF16), shp(D_ATTN, F32), shp(D_ATTN, F32),
                   shp(D_ATTN, BF16), shp(D_ATTN, BF16), shp(N_IDX_HEADS * D_IDX, BF16), shp(LANES, F32),
                   shp(LANES, BF16)],
        compiler_params=_cparams(("arbitrary",)),
        name="inproj",
    )(x, mod, mod, w_in_pad)


def _score_to_key(score, keep):
    bits = pltpu.bitcast(score + 0.0, I32)
    key = jnp.where(bits < 0, bits ^ INT_MAX, bits)
    return jnp.where(keep, key, INT_MIN)


def _select_bias(key_ref, bias_ref, j_ref, nchunks, tk, topk, index_bits):
    rows = key_ref.shape[0]
    nl = tk // LANES

    def count(pred):
        def body(c, cnt):
            ks = pl.multiple_of(c * tk, tk)
            m = pred(key_ref[:, pl.ds(ks, tk)], ks).astype(I32)
            part = m[:, 0:LANES]
            for i in range(1, nl):
                part = part + m[:, i * LANES:(i + 1) * LANES]
            return cnt + part
        cnt = lax.fori_loop(0, nchunks, body, jnp.zeros((rows, LANES), I32))
        return jnp.sum(cnt, axis=1, keepdims=True)

    def bit_body(i, carry):
        x, cnt_x = carry
        cand = x + lax.shift_left(jnp.int32(1), 31 - i)
        cnt = count(lambda kk, ks: kk >= cand)
        ok = cnt >= topk
        return jnp.where(ok, cand, x), jnp.where(ok, cnt, cnt_x)

    x0 = jnp.full((rows, 1), INT_MIN, I32)
    x, cnt_x = lax.fori_loop(0, 32, bit_body, (x0, jnp.full((rows, 1), INT_MAX, I32)))

    excess = (cnt_x > topk) & (x > INT_MIN)
    j_ref[...] = jnp.full(j_ref.shape, INT_MAX, I32)

    @pl.when(jnp.max(excess.astype(I32)) > 0)
    def _():
        need = topk - count(lambda kk, ks: kk > x)
        col = lax.broadcasted_iota(I32, (rows, tk), 1)

        def idx_body(i, j):
            cand = j + lax.shift_left(jnp.int32(1), index_bits - 1 - i)
            cnt = count(lambda kk, ks: (kk == x) & (col + ks < cand))
            return jnp.where(cnt < need, cand, j)

        j = lax.fori_loop(0, index_bits, idx_body, jnp.zeros((rows, 1), I32))
        j_ref[...] = jnp.broadcast_to(jnp.where(excess, j, INT_MAX), j_ref.shape)

    xs = jnp.maximum(x, INT_MIN + 1)
    jmax = j_ref[:, 0:1]
    col = lax.broadcasted_iota(I32, (rows, tk), 1)

    def bias_body(c, carry):
        ks = pl.multiple_of(c * tk, tk)
        kk = key_ref[:, pl.ds(ks, tk)]
        sel = (kk > xs) | ((kk == xs) & (col + ks <= jmax))
        bias_ref[:, pl.ds(ks, tk)] = jnp.where(sel, 0.0, MASK_BIAS)
        return carry

    lax.fori_loop(0, nchunks, bias_body, 0)


def _attn_kernel(qi_ref, kiwiq_ref, kiwib_ref, q_ref, k_ref, v_ref, o_ref, key_ref, bias_ref, j_ref,
                 *, tq, tk, topk, index_bits):
    t0 = pl.program_id(1) * tq
    nkc = (t0 + tq + tk - 1) // tk
    wi = kiwiq_ref[:, D_IDX:D_IDX + N_IDX_HEADS] * IDX_SCALE
    qi = qi_ref[...]
    row = lax.broadcasted_iota(I32, (tq, tk), 0) + t0
    col = lax.broadcasted_iota(I32, (tq, tk), 1)

    def score_body(c, carry):
        ks = pl.multiple_of(c * tk, tk)
        ki = kiwib_ref[pl.ds(ks, tk), :][:, 0:D_IDX]
        acc = jnp.zeros((tq, tk), F32)
        for h in range(N_IDX_HEADS):
            s = _dot_nt(qi[:, h * D_IDX:(h + 1) * D_IDX], ki)
            acc = acc + jnp.maximum(s, 0.0) * wi[:, h:h + 1]
        key_ref[:, pl.ds(ks, tk)] = _score_to_key(acc, col + ks <= row)
        return carry

    lax.fori_loop(0, nkc, score_body, 0)
    _select_bias(key_ref, bias_ref, j_ref, nkc, tk, topk, index_bits)

    q = q_ref[...]
    tmax = (t0 + tq - 1).astype(F32)
    colf = lax.broadcasted_iota(I32, (1, tk), 1).astype(F32)
    outs = []
    for h in range(N_HEADS):
        qh = q[:, h * HEAD_DIM:(h + 1) * HEAD_DIM]
        slope = ALIBI_SLOPES[h]

        def body(c, carry, qh=qh, slope=slope, h=h):
            m, l, acc = carry
            ks = pl.multiple_of(c * tk, tk)
            kh = k_ref[pl.ds(ks, tk), h * HEAD_DIM:(h + 1) * HEAD_DIM]
            vh = v_ref[pl.ds(ks, tk), h * HEAD_DIM:(h + 1) * HEAD_DIM]
            s = _dot_nt(qh, kh) + bias_ref[:, pl.ds(ks, tk)] + slope * (colf + (ks.astype(F32) - tmax))
            m_new = jnp.maximum(m, jnp.max(s, axis=1, keepdims=True))
            a = jnp.exp(m - m_new)
            p = jnp.exp(s - m_new)
            l = a * l + jnp.sum(p, axis=1, keepdims=True)
            acc = a * acc + _dot(p.astype(BF16), vh)
            return m_new, l, acc

        m, l, acc = lax.fori_loop(
            0, nkc, body,
            (jnp.full((tq, 1), MASK_BIAS, F32), jnp.zeros((tq, 1), F32), jnp.zeros((tq, HEAD_DIM), F32)))
        outs.append(acc / l)
    o_ref[...] = jnp.concatenate(outs, axis=1).astype(o_ref.dtype)


def _attn_prompt(qi, kiwi, kiwib, q, kb, vb, *, tq, tk):
    b, s, _ = q.shape
    topk = min(TOPK_MAX, s // 4)
    blk = lambda w: pl.BlockSpec((None, tq, w), lambda bi, i: (bi, i, 0))
    full = lambda w: pl.BlockSpec((None, s, w), lambda bi, i: (bi, 0, 0))
    return pl.pallas_call(
        functools.partial(_attn_kernel, tq=tq, tk=tk, topk=topk, index_bits=max(s - 1, 1).bit_length()),
        grid=(b, s // tq),
        in_specs=[blk(N_IDX_HEADS * D_IDX), blk(LANES), full(LANES), blk(D_ATTN), full(D_ATTN), full(D_ATTN)],
        out_specs=blk(D_ATTN),
        out_shape=jax.ShapeDtypeStruct((b, s, D_ATTN), BF16),
        scratch_shapes=[pltpu.VMEM((tq, s), I32), pltpu.VMEM((tq, s), F32), pltpu.VMEM((tq, LANES), I32)],
        compiler_params=_cparams(("arbitrary", "arbitrary")),
        name="attn_prompt",
    )(qi, kiwi, kiwib, q, kb, vb)


def _pool_window_per_lane():
    lane = lax.broadcasted_iota(I32, (1, D_POOL), 1)
    return lax.shift_left(jnp.int32(POOL_WINDOWS[0]), lane // POOL_GW)


def _mixout_kernel(x_ref, vglu_ref, upool_ref, yattn_ref, wdw_ref, bdw_ref, gcln_ref, bcln_ref, wpool_ref,
                   spool_ref, wout_ref, g_ref, lng_ref, lnb_ref, o_ref, extc_ref, extp_ref, *, tt, alpha):
    ti = pl.program_id(1)
    t0 = pl.multiple_of(ti * tt, tt)
    hs = pl.multiple_of(jnp.maximum(t0 - HALO, 0), 8)
    first = ti == 0
    extc_ref[0:HALO, :] = jnp.where(first, 0.0, vglu_ref[pl.ds(hs, HALO), :])
    extc_ref[HALO:, :] = vglu_ref[pl.ds(t0, tt), :]
    extp_ref[0:HALO, :] = jnp.where(first, 0.0, upool_ref[pl.ds(hs, HALO), :])
    u = upool_ref[pl.ds(t0, tt), :]
    extp_ref[HALO:, :] = u

    y = jnp.broadcast_to(bdw_ref[...], (tt, D_CONV))
    for j in range(CONV_W):
        y = y + extc_ref[pl.ds(HALO - CONV_STATE + j, tt), :] * wdw_ref[j:j + 1, :]
    yconv = _silu(_layer_norm(y, gcln_ref[...], bcln_ref[...]))

    wl = _pool_window_per_lane()
    win = u
    for i in range(1, max(POOL_WINDOWS)):
        win = win + extp_ref[pl.ds(HALO - i, tt), :] * (i < wl).astype(F32)
    pos = lax.broadcasted_iota(I32, (tt, 1), 0) + t0
    cnt = jnp.minimum(pos + 1, wl).astype(F32)
    d = win / cnt - u
    ypool = _dot(d.astype(BF16), wpool_ref[...]) * spool_ref[...]

    cat = jnp.concatenate([yconv.astype(BF16), ypool.astype(BF16), yattn_ref[...]], axis=1)
    y = _dot(cat, wout_ref[...])
    o_ref[...] = _layer_norm(alpha * x_ref[...] + g_ref[...] * y, lng_ref[...], lnb_ref[...])


def _mixout_prompt(x, vglu, upool, yattn, mod, wdw, bdw, gcln, bcln, wpool, spool, wout, lng, lnb, *, tt, alpha):
    b, s, d = x.shape
    blk = lambda w: pl.BlockSpec((None, tt, w), lambda bi, i: (bi, i, 0))
    full = lambda w: pl.BlockSpec((None, s, w), lambda bi, i: (bi, 0, 0))
    const = lambda a: pl.BlockSpec(a.shape, lambda bi, i: (0,) * a.ndim)
    return pl.pallas_call(
        functools.partial(_mixout_kernel, tt=tt, alpha=alpha),
        grid=(b, s // tt),
        in_specs=[blk(d), full(D_CONV), full(D_POOL), blk(D_ATTN), const(wdw), const(bdw), const(gcln),
                  const(bcln), const(wpool), const(spool), const(wout),
                  pl.BlockSpec((None, 1, d), lambda bi, i: (bi, 0, 5)), const(lng), const(lnb)],
        out_specs=blk(d),
        out_shape=jax.ShapeDtypeStruct((b, s, d), F32),
        scratch_shapes=[pltpu.VMEM((HALO + tt, D_CONV), F32), pltpu.VMEM((HALO + tt, D_POOL), F32)],
        compiler_params=_cparams(("arbitrary", "arbitrary")),
        name="mixout_prompt",
    )(x, vglu, upool, yattn, wdw, bdw, gcln, bcln, wpool, spool, wout, mod, lng, lnb)


def _sample_score_kernel(pt_ref, qi_ref, wi_ref, kinew_ref, *rest, n_pages):
    page_refs, o_ref = rest[:n_pages], rest[n_pages]
    qi = qi_ref[...]
    wi = wi_ref[...] * IDX_SCALE
    for p in range(n_pages):
        s = _dot_nt(qi, page_refs[p][...].astype(BF16))
        o_ref[p:p + 1, :] = jnp.sum(jnp.maximum(s, 0.0) * wi, axis=0, keepdims=True)
    kn = kinew_ref[...].astype(BF16).astype(F32)
    sn = jnp.sum(qi.astype(F32) * kn, axis=1, keepdims=True)
    snew = jnp.sum(jnp.maximum(sn, 0.0) * wi, axis=0, keepdims=True)
    lane = lax.broadcasted_iota(I32, (1, PAGE_SIZE), 1)
    o_ref[n_pages:n_pages + 1, :] = jnp.where(lane == 0, snew, -jnp.inf)


def _sample_scores(page_table, qi3, wi3, kinew3, cache_kidx, layer):
    bd, n_pages = page_table.shape
    page_spec = lambda p: pl.BlockSpec((None, None, PAGE_SIZE, D_IDX), lambda b, pt: (layer, pt[b, p], 0, 0))
    return pl.pallas_call(
        functools.partial(_sample_score_kernel, n_pages=n_pages),
        grid_spec=pltpu.PrefetchScalarGridSpec(
            num_scalar_prefetch=1,
            grid=(bd,),
            in_specs=[pl.BlockSpec((None, N_IDX_HEADS, D_IDX), lambda b, pt: (b, 0, 0)),
                      pl.BlockSpec((None, N_IDX_HEADS, 1), lambda b, pt: (b, 0, 0)),
                      pl.BlockSpec((None, 1, D_IDX), lambda b, pt: (b, 0, 0))]
            + [page_spec(p) for p in range(n_pages)],
            out_specs=pl.BlockSpec((None, n_pages + 1, PAGE_SIZE), lambda b, pt: (b, 0, 0)),
        ),
        out_shape=jax.ShapeDtypeStruct((bd, n_pages + 1, PAGE_SIZE), F32),
        compiler_params=_cparams(("arbitrary",)),
        name="sample_scores",
    )(page_table, qi3, wi3, kinew3, *([cache_kidx] * n_pages))


def _sample_select_kernel(s_ref, bias_ref, key_ref, j_ref, *, topk, index_bits):
    s = s_ref[...]
    key_ref[...] = _score_to_key(s, s > -jnp.inf)
    _select_bias(key_ref, bias_ref, j_ref, s.shape[1] // LANES, LANES, topk, index_bits)


def _sample_select(scores2d, topk):
    bd, n = scores2d.shape
    return pl.pallas_call(
        functools.partial(_sample_select_kernel, topk=topk, index_bits=max(n - 1, 1).bit_length()),
        out_shape=jax.ShapeDtypeStruct((bd, n), F32),
        scratch_shapes=[pltpu.VMEM((bd, n), I32), pltpu.VMEM((bd, LANES), I32)],
        compiler_params=pltpu.CompilerParams(vmem_limit_bytes=VMEM_LIMIT),
        name="sample_select",
    )(scores2d)


def _sample_attn_kernel(pt_ref, q_ref, knew_ref, vnew_ref, bias_ref, *rest, pps, n_pages):
    k_refs, v_refs, o_ref = rest[:pps], rest[pps:2 * pps], rest[2 * pps]
    m_ref, l_ref, acc_ref = rest[2 * pps + 1:]
    step = pl.program_id(1)
    past = n_pages * PAGE_SIZE

    @pl.when(step == 0)
    def _():
        m_ref[...] = jnp.full(m_ref.shape, MASK_BIAS, F32)
        l_ref[...] = jnp.zeros_like(l_ref)
        acc_ref[...] = jnp.zeros_like(acc_ref)

    hrow = lax.broadcasted_iota(I32, (N_HEADS, D_ATTN), 0)
    hcol = lax.broadcasted_iota(I32, (N_HEADS, D_ATTN), 1) // HEAD_DIM
    own = hrow == hcol
    qbd = jnp.where(own, jnp.broadcast_to(q_ref[...], (N_HEADS, D_ATTN)), 0.0)
    qbd_b = qbd.astype(BF16)
    slopes = jnp.exp2(-8.0 * (lax.broadcasted_iota(I32, (N_HEADS, 1), 0) + 1).astype(F32) / N_HEADS)
    lane = lax.broadcasted_iota(I32, (1, PAGE_SIZE), 1)

    def update(s, pv_fn):
        m = m_ref[...]
        m_new = jnp.maximum(m, jnp.max(s, axis=1, keepdims=True))
        a = jnp.exp(m - m_new)
        p = jnp.exp(s - m_new)
        l_ref[...] = a * l_ref[...] + jnp.sum(p, axis=1, keepdims=True)
        acc_ref[...] = a * acc_ref[...] + pv_fn(p)
        m_ref[...] = m_new

    for i in range(pps):
        page = step * pps + i
        dist = (past - (page * PAGE_SIZE + lane)).astype(F32)
        bias = bias_ref[pl.ds(page, 1), :]
        s = _dot_nt(qbd_b, k_refs[i][...].astype(BF16)) - slopes * dist + bias
        vb = v_refs[i][...].astype(BF16)
        update(s, lambda p, vb=vb: _dot(p.astype(BF16), vb))

    @pl.when(step == pl.num_programs(1) - 1)
    def _():
        sn = jnp.sum(qbd * knew_ref[...], axis=1, keepdims=True) + bias_ref[n_pages:n_pages + 1, 0:1]
        update(sn, lambda p: p * vnew_ref[...])
        out = jnp.where(own, acc_ref[...] / l_ref[...], 0.0)
        o_ref[...] = jnp.sum(out, axis=0, keepdims=True).astype(o_ref.dtype)


def _sample_attn(page_table, q3, knew3, vnew3, bias3, cache_k, cache_v, layer, *, pps):
    bd, n_pages = page_table.shape
    page_spec = lambda i: pl.BlockSpec((None, None, PAGE_SIZE, D_ATTN),
                                       lambda b, j, pt: (layer, pt[b, j * pps + i], 0, 0))
    vec = pl.BlockSpec((None, 1, D_ATTN), lambda b, j, pt: (b, 0, 0))
    return pl.pallas_call(
        functools.partial(_sample_attn_kernel, pps=pps, n_pages=n_pages),
        grid_spec=pltpu.PrefetchScalarGridSpec(
            num_scalar_prefetch=1,
            grid=(bd, n_pages // pps),
            in_specs=[vec, vec, vec,
                      pl.BlockSpec((None, n_pages + 1, PAGE_SIZE), lambda b, j, pt: (b, 0, 0))]
            + [page_spec(i) for i in range(pps)] * 2,
            out_specs=vec,
            scratch_shapes=[pltpu.VMEM((N_HEADS, 1), F32), pltpu.VMEM((N_HEADS, 1), F32),
                            pltpu.VMEM((N_HEADS, D_ATTN), F32)],
        ),
        out_shape=jax.ShapeDtypeStruct((bd, 1, D_ATTN), BF16),
        compiler_params=_cparams(("arbitrary", "arbitrary")),
        name="sample_attn",
    )(page_table, q3, knew3, vnew3, bias3, *([cache_k] * pps), *([cache_v] * pps))


def _mixout_sample_kernel(x_ref, vnew_ref, unew_ref, yattn_ref, stc_ref, stp_ref, wdw_ref, bdw_ref, gcln_ref,
                          bcln_ref, wpool_ref, spool_ref, wout_ref, g_ref, lng_ref, lnb_ref, o_ref,
                          *, alpha, pos0):
    vnew = vnew_ref[...]
    unew = unew_ref[...]
    w = wdw_ref[...]
    y = (bdw_ref[...] + jnp.sum(stc_ref[...] * w[0:CONV_STATE][None], axis=1)
         + vnew * w[CONV_STATE:CONV_STATE + 1])
    yconv = _silu(_layer_norm(y, gcln_ref[...], bcln_ref[...]))

    wl = _pool_window_per_lane()
    age = POOL_STATE - lax.broadcasted_iota(I32, (POOL_STATE, D_POOL), 0)
    inwin = (age < wl).astype(F32)
    win = unew + jnp.sum(stp_ref[...] * inwin[None], axis=1)
    cnt = jnp.minimum(pos0 + 1, wl).astype(F32)
    d = win / cnt - unew
    ypool = _dot(d.astype(BF16), wpool_ref[...]) * spool_ref[...]

    cat = jnp.concatenate([yconv.astype(BF16), ypool.astype(BF16), yattn_ref[...]], axis=1)
    y = _dot(cat, wout_ref[...])
    o_ref[...] = _layer_norm(alpha * x_ref[...] + g_ref[...] * y, lng_ref[...], lnb_ref[...])


def _mixout_sample(x, vnew, unew, yattn, state_conv, state_pool, layer, mod, wdw, bdw, gcln, bcln, wpool, spool,
                   wout, lng, lnb, *, alpha, pos0):
    bd, d = x.shape
    const = lambda a: pl.BlockSpec(a.shape, lambda i: (0,) * a.ndim)
    st = lambda a: pl.BlockSpec((None,) + a.shape[1:], lambda i: (layer, 0, 0, 0))
    return pl.pallas_call(
        functools.partial(_mixout_sample_kernel, alpha=alpha, pos0=pos0),
        grid=(1,),
        in_specs=[const(x), const(vnew), const(unew), const(yattn), st(state_conv), st(state_pool), const(wdw),
                  const(bdw), const(gcln), const(bcln), const(wpool), const(spool), const(wout),
                  pl.BlockSpec((None, bd, d), lambda i: (0, 0, 5)), const(lng), const(lnb)],
        out_specs=const(x),
        out_shape=jax.ShapeDtypeStruct((bd, d), F32),
        compiler_params=_cparams(("arbitrary",)),
        name="mixout_sample",
    )(x, vnew, unew, yattn, state_conv, state_pool, wdw, bdw, gcln, bcln, wpool, spool, wout, mod, lng, lnb)


def _block_diag(w):
    g, a, b = w.shape
    out = jnp.zeros((g * a, g * b), w.dtype)
    for i in range(g):
        out = out.at[i * a:(i + 1) * a, i * b:(i + 1) * b].set(w[i])
    return out


def _tile(n, pref):
    t = min(n, pref)
    while n % t:
        t //= 2
    return t


def kernel(x_prompt, x_sample, cache_k, cache_v, cache_kidx, state_conv, state_pool, page_table, c_prompt, c_sample,
           w_ada, b_ada, ln_g, ln_b, w_ff_in, w_ff_out, w_in, w_dw, b_dw, g_conv_ln, b_conv_ln, w_pool, s_pool,
           w_out):
    bp, s, d = x_prompt.shape
    bd, t_dec, _ = x_sample.shape
    assert t_dec == 1, "the sample path handles one decode position per sequence"
    depth = w_ada.shape[0]
    n_phys = cache_k.shape[1]
    n_pages = page_table.shape[1]
    past = n_pages * PAGE_SIZE
    alpha = (2 * depth) ** 0.25
    topk_s = min(TOPK_MAX, (past + t_dec) // 4)

    r = bp + bd
    r_pad = -(-r // 8) * 8
    c_all = jnp.concatenate([c_prompt, c_sample, jnp.zeros((r_pad - r, d), F32)], axis=0)
    mod_all = _ada(c_all, w_ada, b_ada)

    wff_in = w_ff_in.astype(BF16)
    wff_out = w_ff_out.astype(BF16)
    w_in_pad = jnp.pad(w_in, ((0, 0), (0, 0), (0, D_IN_PAD - D_IN))).astype(BF16)
    wout_b = w_out.astype(BF16)
    wdw_pad = jnp.pad(w_dw, ((0, 0), (0, HALO - CONV_W), (0, 0)))
    ck = cache_k.reshape(depth, n_phys, PAGE_SIZE, D_ATTN)
    cv = cache_v.reshape(depth, n_phys, PAGE_SIZE, D_ATTN)

    tm = _tile(s, 512)
    tf = 256
    tq = _tile(s, 128)
    tk = _tile(s, 512)
    pps = _tile(n_pages, 8)

    xp = x_prompt.reshape(bp * s, d)
    xs = x_sample.reshape(bd, d)
    outs_p = [[] for _ in range(5)]
    outs_s = [[] for _ in range(5)]
    for l in range(depth):
        mod_p = mod_all[l, :bp].reshape(bp, 1, 9 * d)
        mod_s = mod_all[l, bp:r].reshape(1, bd, 9 * d)
        lng = [ln_g[l, i].reshape(1, d) for i in range(3)]
        lnb = [ln_b[l, i].reshape(1, d) for i in range(3)]
        mix_w = (wdw_pad[l], b_dw[l].reshape(1, D_CONV), g_conv_ln[l].reshape(1, D_CONV),
                 b_conv_ln[l].reshape(1, D_CONV), _block_diag(w_pool[l]).astype(BF16),
                 s_pool[l].reshape(1, D_POOL), wout_b[l])

        kw = dict(per_row=False, rows_per_seq=s, tm=tm)
        xp = _ffn(xp, mod_p, 0, wff_in[l, 0], wff_out[l, 0], lng[0], lnb[0], alpha=alpha, tf=tf, **kw)
        vglu, upool, q, k, v, kb, vb, qi, kiwi, kiwib = _inproj(xp, mod_p, w_in_pad[l], **kw)
        sq = lambda a: a.reshape(bp, s, a.shape[-1])
        yattn = _attn_prompt(sq(qi), sq(kiwi), sq(kiwib), sq(q), sq(kb), sq(vb), tq=tq, tk=tk)
        xp = _mixout_prompt(sq(xp), sq(vglu), sq(upool), yattn, mod_p, *mix_w, lng[1], lnb[1],
                            tt=tm, alpha=alpha).reshape(bp * s, d)
        xp = _ffn(xp, mod_p, 6, wff_in[l, 1], wff_out[l, 1], lng[2], lnb[2], alpha=alpha, tf=tf, **kw)
        outs_p[0].append(k.reshape(bp, s, N_HEADS, HEAD_DIM))
        outs_p[1].append(v.reshape(bp, s, N_HEADS, HEAD_DIM))
        outs_p[2].append(sq(kiwi)[:, :, :D_IDX])
        outs_p[3].append(sq(vglu)[:, s - CONV_STATE:])
        outs_p[4].append(sq(upool)[:, s - POOL_STATE:])

        kw = dict(per_row=True, rows_per_seq=1, tm=bd)
        xs = _ffn(xs, mod_s, 0, wff_in[l, 0], wff_out[l, 0], lng[0], lnb[0], alpha=alpha, tf=tf, **kw)
        vglu, upool, q, k, v, kb, vb, qi, kiwi, kiwib = _inproj(xs, mod_s, w_in_pad[l], **kw)
        scores = _sample_scores(page_table, qi.reshape(bd, N_IDX_HEADS, D_IDX),
                                kiwi[:, D_IDX:D_IDX + N_IDX_HEADS].reshape(bd, N_IDX_HEADS, 1),
                                kiwi[:, :D_IDX].reshape(bd, 1, D_IDX), cache_kidx, l)
        bias = _sample_select(scores.reshape(bd, (n_pages + 1) * PAGE_SIZE), topk_s)
        yattn = _sample_attn(page_table, q.astype(F32).reshape(bd, 1, D_ATTN), k.reshape(bd, 1, D_ATTN),
                             v.reshape(bd, 1, D_ATTN), bias.reshape(bd, n_pages + 1, PAGE_SIZE), ck, cv, l,
                             pps=pps).reshape(bd, D_ATTN)
        xs = _mixout_sample(xs, vglu, upool, yattn, state_conv, state_pool, l, mod_s, *mix_w, lng[1], lnb[1],
                            alpha=alpha, pos0=past)
        xs = _ffn(xs, mod_s, 6, wff_in[l, 1], wff_out[l, 1], lng[2], lnb[2], alpha=alpha, tf=tf, **kw)
        outs_s[0].append(k.reshape(bd, 1, N_HEADS, HEAD_DIM))
        outs_s[1].append(v.reshape(bd, 1, N_HEADS, HEAD_DIM))
        outs_s[2].append(kiwi[:, :D_IDX].reshape(bd, 1, D_IDX))
        outs_s[3].append(jnp.concatenate([state_conv[l][:, 1:], vglu[:, None, :]], axis=1))
        outs_s[4].append(jnp.concatenate([state_pool[l][:, 1:], upool[:, None, :]], axis=1))

    return (xp.reshape(bp, s, d), xs.reshape(bd, 1, d),
            *[jnp.stack(o) for o in outs_p], *[jnp.stack(o) for o in outs_s])
```

```python
import functools

import jax
import jax.numpy as jnp
from jax import lax
from jax.experimental import pallas as pl
from jax.experimental.pallas import tpu as pltpu

F32 = jnp.float32
BF16 = jnp.bfloat16
I32 = jnp.int32

D_CONV = 256
CONV_W = 31
CONV_STATE = CONV_W - 1
D_POOL = 256
POOL_WINDOWS = (2, 4, 8, 16)
POOL_GW = D_POOL // len(POOL_WINDOWS)
POOL_STATE = max(POOL_WINDOWS) - 1
N_HEADS = 8
HEAD_DIM = 64
D_ATTN = N_HEADS * HEAD_DIM
N_IDX_HEADS = 8
D_IDX = 64
D_QI = N_IDX_HEADS * D_IDX
TOPK_MAX = 256
PAGE_SIZE = 128
LN_EPS = 1e-5
D_IN = 2 * D_CONV + D_POOL + 3 * D_ATTN + D_QI + D_IDX + N_IDX_HEADS

LANES = 128
ROW_Q = 2 * D_CONV + D_POOL
ROW_K = ROW_Q + D_ATTN
ROW_V = ROW_K + D_ATTN
ROW_QI = ROW_V + D_ATTN
ROW_KI = ROW_QI + D_QI
D_IN_PAD = ROW_KI + LANES
HALO = 32
INT_MIN = -2 ** 31
INT_MAX = 2 ** 31 - 1
MASK_BIAS = -1e30
VMEM_LIMIT = 56 * 1024 * 1024

ALIBI_SLOPES = tuple(2.0 ** (-8.0 * (h + 1) / N_HEADS) for h in range(N_HEADS))
IDX_SCALE = (D_IDX ** -0.5) * (N_IDX_HEADS ** -0.5)


def _layer_norm(y, g, b):
    mu = jnp.mean(y, axis=-1, keepdims=True)
    d = y - mu
    var = jnp.mean(d * d, axis=-1, keepdims=True)
    return d * lax.rsqrt(var + LN_EPS) * g + b


def _silu(x):
    return x * jax.nn.sigmoid(x)


def _dot(a, b):
    return jnp.dot(a, b, preferred_element_type=F32)


def _dot_nt(a, b):
    return lax.dot_general(a, b, (((1,), (1,)), ((), ())), preferred_element_type=F32)


def _cparams(sem):
    return pltpu.CompilerParams(dimension_semantics=sem, vmem_limit_bytes=VMEM_LIMIT)


def _ada_kernel(c_ref, w_ref, b_ref, o_ref):
    s = _silu(c_ref[...]).astype(BF16)
    o_ref[...] = _dot(s, w_ref[...].astype(BF16)) + b_ref[...]


def _ada(c_all, w_ada, b_ada):
    depth, d, nd = w_ada.shape
    r = c_all.shape[0]
    return pl.pallas_call(
        _ada_kernel,
        grid=(depth, nd // d),
        in_specs=[
            pl.BlockSpec((r, d), lambda l, j: (0, 0)),
            pl.BlockSpec((None, d, d), lambda l, j: (l, 0, j)),
            pl.BlockSpec((None, 1, d), lambda l, j: (l, 0, j)),
        ],
        out_specs=pl.BlockSpec((None, r, d), lambda l, j: (l, 0, j)),
        out_shape=jax.ShapeDtypeStruct((depth, r, nd), F32),
        compiler_params=_cparams(("arbitrary", "arbitrary")),
        name="ada",
    )(c_all, w_ada, b_ada.reshape(depth, 1, nd))


def _ffn_kernel(x_ref, sh_ref, sc_ref, g_ref, wa_ref, wg_ref, wo_ref, lng_ref, lnb_ref, o_ref,
                acc_ref, h_ref, *, alpha):
    j = pl.program_id(1)

    @pl.when(j == 0)
    def _():
        h_ref[...] = (x_ref[...] * (1.0 + sc_ref[...]) + sh_ref[...]).astype(BF16)
        acc_ref[...] = jnp.zeros_like(acc_ref)

    h = h_ref[...]
    a = _dot(h, wa_ref[...])
    g = _dot(h, wg_ref[...])
    u = (_silu(g) * a).astype(BF16)
    acc_ref[...] += _dot(u, wo_ref[...])

    @pl.when(j == pl.num_programs(1) - 1)
    def _():
        y = alpha * x_ref[...] + 0.5 * g_ref[...] * acc_ref[...]
        o_ref[...] = _layer_norm(y, lng_ref[...], lnb_ref[...])


def _mod_spec(per_row, rows_per_seq, tm, d, col):
    if per_row:
        return pl.BlockSpec((None, tm, d), lambda i, *_: (0, i, col))
    return pl.BlockSpec((None, 1, d), lambda i, *_: ((i * tm) // rows_per_seq, 0, col))


def _ffn(x, mod, col0, wff_in, wff_out, layer, which, lng, lnb, *, alpha, per_row, rows_per_seq, tm, tf):
    n, d = x.shape
    f = wff_out.shape[2]
    nf = f // tf
    ms = functools.partial(_mod_spec, per_row, rows_per_seq, tm, d)
    return pl.pallas_call(
        functools.partial(_ffn_kernel, alpha=alpha),
        grid=(n // tm, nf),
        in_specs=[
            pl.BlockSpec((tm, d), lambda i, j: (i, 0)),
            ms(col0), ms(col0 + 1), ms(col0 + 2),
            pl.BlockSpec((None, None, d, tf), lambda i, j: (layer, which, 0, j)),
            pl.BlockSpec((None, None, d, tf), lambda i, j: (layer, which, 0, j + nf)),
            pl.BlockSpec((None, None, tf, d), lambda i, j: (layer, which, j, 0)),
            pl.BlockSpec((1, d), lambda i, j: (0, 0)),
            pl.BlockSpec((1, d), lambda i, j: (0, 0)),
        ],
        out_specs=pl.BlockSpec((tm, d), lambda i, j: (i, 0)),
        out_shape=jax.ShapeDtypeStruct((n, d), F32),
        scratch_shapes=[pltpu.VMEM((tm, d), F32), pltpu.VMEM((tm, d), BF16)],
        compiler_params=_cparams(("arbitrary", "arbitrary")),
        name="ffn",
    )(x, mod, mod, mod, wff_in, wff_in, wff_out, lng, lnb)


def _inproj_kernel(x_ref, sh_ref, sc_ref, w_ref, vglu_ref, upool_ref, q_ref, qi_ref, kiwi_ref,
                   kt_ref, ktb_ref, vt_ref, vtb_ref, kit_ref, kitb_ref, *rowmajor_kv):
    h = (x_ref[...] * (1.0 + sc_ref[...]) + sh_ref[...]).astype(BF16)
    y = _dot_nt(h, w_ref[0:ROW_K, :])
    vglu_ref[...] = y[:, 0:D_CONV] * jax.nn.sigmoid(y[:, D_CONV:2 * D_CONV])
    upool_ref[...] = y[:, 2 * D_CONV:ROW_Q]
    q_ref[...] = (y[:, ROW_Q:ROW_K] * (HEAD_DIM ** -0.5)).astype(BF16)
    qi_ref[...] = _dot_nt(h, w_ref[ROW_QI:ROW_KI, :]).astype(BF16)
    kiwi_ref[...] = _dot_nt(h, w_ref[ROW_KI:D_IN_PAD, :])
    kt = _dot_nt(w_ref[ROW_K:ROW_V, :], h)
    kt_ref[...] = kt
    ktb_ref[...] = kt.astype(BF16)
    vt = _dot_nt(w_ref[ROW_V:ROW_QI, :], h)
    vt_ref[...] = vt
    vtb_ref[...] = vt.astype(BF16)
    kit = _dot_nt(w_ref[ROW_KI:ROW_KI + D_IDX, :], h)
    kit_ref[...] = kit
    kitb_ref[...] = kit.astype(BF16)
    if rowmajor_kv:
        k_ref, v_ref = rowmajor_kv
        k_ref[...] = _dot_nt(h, w_ref[ROW_K:ROW_V, :])
        v_ref[...] = _dot_nt(h, w_ref[ROW_V:ROW_QI, :])


def _inproj(x, mod, w_in_t, layer, *, per_row, rows_per_seq, tm, rowmajor_kv):
    n, d = x.shape
    g = max(n // rows_per_seq, 1) if not per_row else 1
    rows = n // g
    tpb = rows // tm
    ms = functools.partial(_mod_spec, per_row, rows_per_seq, tm, d)
    row = lambda w: pl.BlockSpec((tm, w), lambda i: (i, 0))
    shp = lambda w, dt: jax.ShapeDtypeStruct((n, w), dt)
    tr = lambda w: pl.BlockSpec((None, w, tm), lambda i: (i // tpb, 0, i % tpb))
    tshp = lambda w, dt: jax.ShapeDtypeStruct((g, w, rows), dt)
    out_specs = [row(D_CONV), row(D_POOL), row(D_ATTN), row(D_QI), row(LANES),
                 tr(D_ATTN), tr(D_ATTN), tr(D_ATTN), tr(D_ATTN), tr(D_IDX), tr(D_IDX)]
    out_shape = [shp(D_CONV, F32), shp(D_POOL, F32), shp(D_ATTN, BF16), shp(D_QI, BF16), shp(LANES, F32),
                 tshp(D_ATTN, F32), tshp(D_ATTN, BF16), tshp(D_ATTN, F32), tshp(D_ATTN, BF16),
                 tshp(D_IDX, F32), tshp(D_IDX, BF16)]
    if rowmajor_kv:
        out_specs += [row(D_ATTN), row(D_ATTN)]
        out_shape += [shp(D_ATTN, F32), shp(D_ATTN, F32)]
    return pl.pallas_call(
        _inproj_kernel,
        grid=(n // tm,),
        in_specs=[row(d), ms(3), ms(4), pl.BlockSpec((None, D_IN_PAD, d), lambda i: (layer, 0, 0))],
        out_specs=out_specs,
        out_shape=out_shape,
        compiler_params=_cparams(("arbitrary",)),
        name="inproj",
    )(x, mod, mod, w_in_t)


def _score_to_key(score, keep):
    bits = pltpu.bitcast(score + 0.0, I32)
    key = jnp.where(bits < 0, bits ^ INT_MAX, bits)
    return jnp.where(keep, key, INT_MIN)


def _select_bias(key_ref, bias_ref, j_ref, nchunks, tk, topk, index_bits):
    rows = key_ref.shape[0]
    nl = tk // LANES

    def count(pred):
        def body(c, cnt):
            ks = pl.multiple_of(c * tk, tk)
            m = pred(key_ref[:, pl.ds(ks, tk)], ks).astype(I32)
            part = m[:, 0:LANES]
            for i in range(1, nl):
                part = part + m[:, i * LANES:(i + 1) * LANES]
            return cnt + part
        cnt = lax.fori_loop(0, nchunks, body, jnp.zeros((rows, LANES), I32))
        return jnp.sum(cnt, axis=1, keepdims=True)

    def bit_body(i, carry):
        x, cnt_x = carry
        cand = x + lax.shift_left(jnp.int32(1), 31 - i)
        cnt = count(lambda kk, ks: kk >= cand)
        ok = cnt >= topk
        return jnp.where(ok, cand, x), jnp.where(ok, cnt, cnt_x)

    x0 = jnp.full((rows, 1), INT_MIN, I32)
    x, cnt_x = lax.fori_loop(0, 32, bit_body, (x0, jnp.full((rows, 1), INT_MAX, I32)))

    excess = (cnt_x > topk) & (x > INT_MIN)
    j_ref[...] = jnp.full(j_ref.shape, INT_MAX, I32)

    @pl.when(jnp.max(excess.astype(I32)) > 0)
    def _():
        need = topk - count(lambda kk, ks: kk > x)
        col = lax.broadcasted_iota(I32, (rows, tk), 1)

        def idx_body(i, j):
            cand = j + lax.shift_left(jnp.int32(1), index_bits - 1 - i)
            cnt = count(lambda kk, ks: (kk == x) & (col + ks < cand))
            return jnp.where(cnt < need, cand, j)

        j = lax.fori_loop(0, index_bits, idx_body, jnp.zeros((rows, 1), I32))
        j_ref[...] = jnp.broadcast_to(jnp.where(excess, j, INT_MAX), j_ref.shape)

    xs = jnp.maximum(x, INT_MIN + 1)
    jmax = j_ref[:, 0:1]
    col = lax.broadcasted_iota(I32, (rows, tk), 1)

    def bias_body(c, carry):
        ks = pl.multiple_of(c * tk, tk)
        kk = key_ref[:, pl.ds(ks, tk)]
        sel = (kk > xs) | ((kk == xs) & (col + ks <= jmax))
        bias_ref[:, pl.ds(ks, tk)] = jnp.where(sel, 0.0, MASK_BIAS)
        return carry

    lax.fori_loop(0, nchunks, bias_body, 0)


def _lane_tile_fold(x, op):
    out = x[:, 0:LANES]
    for i in range(1, x.shape[1] // LANES):
        out = op(out, x[:, i * LANES:(i + 1) * LANES])
    return out


def _attn_kernel(qi_ref, kiwiq_ref, kit_ref, q_ref, kt_ref, vt_ref, o_ref, key_ref, bias_ref, s_ref, j_ref,
                 *, tq, tk, topk, index_bits):
    t0 = pl.program_id(1) * tq
    nkc = (t0 + tq + tk - 1) // tk
    wi = kiwiq_ref[:, D_IDX:D_IDX + N_IDX_HEADS] * IDX_SCALE
    qi = qi_ref[...]
    row = lax.broadcasted_iota(I32, (tq, tk), 0) + t0
    col = lax.broadcasted_iota(I32, (tq, tk), 1)

    def score_body(c, carry):
        ks = pl.multiple_of(c * tk, tk)
        ki = kit_ref[:, pl.ds(ks, tk)]
        acc = jnp.zeros((tq, tk), F32)
        for h in range(N_IDX_HEADS):
            s = _dot(qi[:, h * D_IDX:(h + 1) * D_IDX], ki)
            acc = acc + jnp.maximum(s, 0.0) * wi[:, h:h + 1]
        key_ref[:, pl.ds(ks, tk)] = _score_to_key(acc, col + ks <= row)
        return carry

    lax.fori_loop(0, nkc, score_body, 0)
    _select_bias(key_ref, bias_ref, j_ref, nkc, tk, topk, index_bits)

    q = q_ref[...]
    qhs = [q[:, h * HEAD_DIM:(h + 1) * HEAD_DIM] for h in range(N_HEADS)]
    heads = [slice(h * HEAD_DIM, (h + 1) * HEAD_DIM) for h in range(N_HEADS)]
    tmax = (t0 + tq - 1).astype(F32)
    colf = lax.broadcasted_iota(I32, (1, tk), 1).astype(F32)

    def logit_body(c, mxs):
        ks = pl.multiple_of(c * tk, tk)
        bias_c = bias_ref[:, pl.ds(ks, tk)]
        rel = colf + (ks.astype(F32) - tmax)
        new = []
        for h in range(N_HEADS):
            s = _dot(qhs[h], kt_ref[heads[h], pl.ds(ks, tk)]) + bias_c + ALIBI_SLOPES[h] * rel
            s_ref[h, :, pl.ds(ks, tk)] = s
            new.append(jnp.maximum(mxs[h], _lane_tile_fold(s, jnp.maximum)))
        return tuple(new)

    mxs = lax.fori_loop(0, nkc, logit_body,
                        tuple(jnp.full((tq, LANES), MASK_BIAS, F32) for _ in range(N_HEADS)))
    ms = [jnp.max(mx, axis=1, keepdims=True) for mx in mxs]

    def pv_body(c, carry):
        lsums, accs = carry
        ks = pl.multiple_of(c * tk, tk)
        new_l, new_a = [], []
        for h in range(N_HEADS):
            p = jnp.exp(s_ref[h, :, pl.ds(ks, tk)] - ms[h])
            new_a.append(accs[h] + _dot_nt(p.astype(BF16), vt_ref[heads[h], pl.ds(ks, tk)]))
            new_l.append(lsums[h] + _lane_tile_fold(p, jnp.add))
        return tuple(new_l), tuple(new_a)

    lsums, accs = lax.fori_loop(
        0, nkc, pv_body,
        (tuple(jnp.zeros((tq, LANES), F32) for _ in range(N_HEADS)),
         tuple(jnp.zeros((tq, HEAD_DIM), F32) for _ in range(N_HEADS))))
    outs = [accs[h] / jnp.sum(lsums[h], axis=1, keepdims=True) for h in range(N_HEADS)]
    o_ref[...] = jnp.concatenate(outs, axis=1).astype(o_ref.dtype)


def _attn_prompt(qi, kiwi, kitb, q, ktb, vtb, *, tq, tk):
    b, s, _ = q.shape
    topk = min(TOPK_MAX, s // 4)
    blk = lambda w: pl.BlockSpec((None, tq, w), lambda bi, i: (bi, i, 0))
    full_t = lambda w: pl.BlockSpec((None, w, s), lambda bi, i: (bi, 0, 0))
    return pl.pallas_call(
        functools.partial(_attn_kernel, tq=tq, tk=tk, topk=topk, index_bits=max(s - 1, 1).bit_length()),
        grid=(b, s // tq),
        in_specs=[blk(D_QI), blk(LANES), full_t(D_IDX), blk(D_ATTN), full_t(D_ATTN), full_t(D_ATTN)],
        out_specs=blk(D_ATTN),
        out_shape=jax.ShapeDtypeStruct((b, s, D_ATTN), BF16),
        scratch_shapes=[pltpu.VMEM((tq, s), I32), pltpu.VMEM((tq, s), F32), pltpu.VMEM((N_HEADS, tq, s), F32),
                        pltpu.VMEM((tq, LANES), I32)],
        compiler_params=_cparams(("arbitrary", "arbitrary")),
        name="attn_prompt",
    )(qi, kiwi, kitb, q, ktb, vtb)


def _pool_window_per_lane():
    lane = lax.broadcasted_iota(I32, (1, D_POOL), 1)
    return lax.shift_left(jnp.int32(POOL_WINDOWS[0]), lane // POOL_GW)


def _mix_tail(x, yconv_pre, d_pool, yattn, gcln_ref, bcln_ref, wpool_ref, spool_ref, wout_ref, g_ref, lng_ref,
              lnb_ref, alpha):
    yconv = _silu(_layer_norm(yconv_pre, gcln_ref[...], bcln_ref[...]))
    ypool = _dot(d_pool.astype(BF16), wpool_ref[...]) * spool_ref[...]
    cat = jnp.concatenate([yconv.astype(BF16), ypool.astype(BF16), yattn], axis=1)
    y = _dot(cat, wout_ref[...])
    return _layer_norm(alpha * x + g_ref[...] * y, lng_ref[...], lnb_ref[...])


def _mixout_kernel(x_ref, vglu_ref, upool_ref, yattn_ref, wdw_ref, bdw_ref, gcln_ref, bcln_ref, wpool_ref,
                   spool_ref, wout_ref, g_ref, lng_ref, lnb_ref, o_ref, extc_ref, extp_ref, *, tt, alpha):
    ti = pl.program_id(1)
    t0 = pl.multiple_of(ti * tt, tt)
    hs = pl.multiple_of(jnp.maximum(t0 - HALO, 0), 8)
    first = ti == 0
    extc_ref[0:HALO, :] = jnp.where(first, 0.0, vglu_ref[pl.ds(hs, HALO), :])
    extc_ref[HALO:, :] = vglu_ref[pl.ds(t0, tt), :]
    extp_ref[0:HALO, :] = jnp.where(first, 0.0, upool_ref[pl.ds(hs, HALO), :])
    u = upool_ref[pl.ds(t0, tt), :]
    extp_ref[HALO:, :] = u

    y = jnp.broadcast_to(bdw_ref[...], (tt, D_CONV))
    for j in range(CONV_W):
        y = y + extc_ref[pl.ds(HALO - CONV_STATE + j, tt), :] * wdw_ref[j:j + 1, :]

    wl = _pool_window_per_lane()
    win = u
    for i in range(1, max(POOL_WINDOWS)):
        win = win + extp_ref[pl.ds(HALO - i, tt), :] * (i < wl).astype(F32)
    pos = lax.broadcasted_iota(I32, (tt, 1), 0) + t0
    cnt = jnp.minimum(pos + 1, wl).astype(F32)
    d = win / cnt - u

    o_ref[...] = _mix_tail(x_ref[...], y, d, yattn_ref[...], gcln_ref, bcln_ref, wpool_ref, spool_ref, wout_ref,
                           g_ref, lng_ref, lnb_ref, alpha)


def _mixout_prompt(x, vglu, upool, yattn, mod, wdw, bdw, gcln, bcln, wpool, spool, wout_b, layer, lng, lnb,
                   *, tt, alpha):
    b, s, d = x.shape
    blk = lambda w: pl.BlockSpec((None, tt, w), lambda bi, i: (bi, i, 0))
    full = lambda w: pl.BlockSpec((None, s, w), lambda bi, i: (bi, 0, 0))
    const = lambda a: pl.BlockSpec(a.shape, lambda bi, i: (0,) * a.ndim)
    return pl.pallas_call(
        functools.partial(_mixout_kernel, tt=tt, alpha=alpha),
        grid=(b, s // tt),
        in_specs=[blk(d), full(D_CONV), full(D_POOL), blk(D_ATTN), const(wdw), const(bdw), const(gcln),
                  const(bcln), const(wpool), const(spool),
                  pl.BlockSpec((None,) + wout_b.shape[1:], lambda bi, i: (layer, 0, 0)),
                  pl.BlockSpec((None, 1, d), lambda bi, i: (bi, 0, 5)), const(lng), const(lnb)],
        out_specs=blk(d),
        out_shape=jax.ShapeDtypeStruct((b, s, d), F32),
        scratch_shapes=[pltpu.VMEM((HALO + tt, D_CONV), F32), pltpu.VMEM((HALO + tt, D_POOL), F32)],
        compiler_params=_cparams(("arbitrary", "arbitrary")),
        name="mixout_prompt",
    )(x, vglu, upool, yattn, wdw, bdw, gcln, bcln, wpool, spool, wout_b, mod, lng, lnb)


def _sample_score_kernel(pt_ref, qi_ref, wi_ref, kinew_ref, *rest, n_pages):
    page_refs, o_ref = rest[:n_pages], rest[n_pages]
    qi = qi_ref[...]
    wi = wi_ref[...] * IDX_SCALE
    for p in range(n_pages):
        s = _dot(qi, page_refs[p][...].astype(BF16))
        o_ref[p:p + 1, :] = jnp.sum(jnp.maximum(s, 0.0) * wi, axis=0, keepdims=True)
    kn = kinew_ref[...].astype(BF16).astype(F32)
    sn = jnp.sum(qi.astype(F32) * kn, axis=1, keepdims=True)
    snew = jnp.sum(jnp.maximum(sn, 0.0) * wi, axis=0, keepdims=True)
    lane = lax.broadcasted_iota(I32, (1, PAGE_SIZE), 1)
    o_ref[n_pages:n_pages + 1, :] = jnp.where(lane == 0, snew, -jnp.inf)


def _sample_scores(page_table, qi3, wi3, kinew3, kidx_t, layer):
    bd, n_pages = page_table.shape
    page_spec = lambda p: pl.BlockSpec((None, None, D_IDX, PAGE_SIZE), lambda b, pt: (layer, pt[b, p], 0, 0))
    return pl.pallas_call(
        functools.partial(_sample_score_kernel, n_pages=n_pages),
        grid_spec=pltpu.PrefetchScalarGridSpec(
            num_scalar_prefetch=1,
            grid=(bd,),
            in_specs=[pl.BlockSpec((None, N_IDX_HEADS, D_IDX), lambda b, pt: (b, 0, 0)),
                      pl.BlockSpec((None, N_IDX_HEADS, 1), lambda b, pt: (b, 0, 0)),
                      pl.BlockSpec((None, 1, D_IDX), lambda b, pt: (b, 0, 0))]
            + [page_spec(p) for p in range(n_pages)],
            out_specs=pl.BlockSpec((None, n_pages + 1, PAGE_SIZE), lambda b, pt: (b, 0, 0)),
        ),
        out_shape=jax.ShapeDtypeStruct((bd, n_pages + 1, PAGE_SIZE), F32),
        compiler_params=_cparams(("arbitrary",)),
        name="sample_scores",
    )(page_table, qi3, wi3, kinew3, *([kidx_t] * n_pages))


def _sample_select_kernel(s_ref, bias_ref, key_ref, j_ref, *, topk, index_bits):
    s = s_ref[...]
    key_ref[...] = _score_to_key(s, s > -jnp.inf)
    _select_bias(key_ref, bias_ref, j_ref, s.shape[1] // LANES, LANES, topk, index_bits)


def _sample_select(scores2d, topk):
    bd, n = scores2d.shape
    return pl.pallas_call(
        functools.partial(_sample_select_kernel, topk=topk, index_bits=max(n - 1, 1).bit_length()),
        out_shape=jax.ShapeDtypeStruct((bd, n), F32),
        scratch_shapes=[pltpu.VMEM((bd, n), I32), pltpu.VMEM((bd, LANES), I32)],
        compiler_params=pltpu.CompilerParams(vmem_limit_bytes=VMEM_LIMIT),
        name="sample_select",
    )(scores2d)


def _sample_attn_kernel(pt_ref, q_ref, knew_ref, vnew_ref, bias_ref, *rest, n_pages):
    kt_refs, vt_refs, o_ref = rest[:n_pages], rest[n_pages:2 * n_pages], rest[2 * n_pages]
    past = n_pages * PAGE_SIZE

    hrow = lax.broadcasted_iota(I32, (N_HEADS, D_ATTN), 0)
    hcol = lax.broadcasted_iota(I32, (N_HEADS, D_ATTN), 1) // HEAD_DIM
    own = hrow == hcol
    qbd = jnp.where(own, jnp.broadcast_to(q_ref[...], (N_HEADS, D_ATTN)), 0.0)
    qbd_b = qbd.astype(BF16)
    slopes = jnp.exp2(-8.0 * (lax.broadcasted_iota(I32, (N_HEADS, 1), 0) + 1).astype(F32) / N_HEADS)

    s = jnp.concatenate([_dot(qbd_b, kt_refs[i][...].astype(BF16)) for i in range(n_pages)], axis=1)
    dist = (past - lax.broadcasted_iota(I32, (1, past), 1)).astype(F32)
    s = s - slopes * dist + bias_ref[:, 0:past]
    sn = jnp.sum(qbd * knew_ref[...], axis=1, keepdims=True) + bias_ref[:, past:past + 1]

    m = jnp.maximum(jnp.max(s, axis=1, keepdims=True), sn)
    p = jnp.exp(s - m)
    pn = jnp.exp(sn - m)
    l = jnp.sum(p, axis=1, keepdims=True) + pn
    pb = p.astype(BF16)
    acc = pn * vnew_ref[...]
    for i in range(n_pages):
        acc = acc + _dot_nt(pb[:, i * PAGE_SIZE:(i + 1) * PAGE_SIZE], vt_refs[i][...].astype(BF16))
    out = jnp.where(own, acc / l, 0.0)
    o_ref[...] = jnp.sum(out, axis=0, keepdims=True).astype(o_ref.dtype)


def _sample_attn(page_table, q3, knew3, vnew3, bias3, ck_t, cv_t, layer):
    bd, n_pages = page_table.shape
    page_spec = lambda i: pl.BlockSpec((None, None, D_ATTN, PAGE_SIZE), lambda b, pt: (layer, pt[b, i], 0, 0))
    vec = pl.BlockSpec((None, 1, D_ATTN), lambda b, pt: (b, 0, 0))
    return pl.pallas_call(
        functools.partial(_sample_attn_kernel, n_pages=n_pages),
        grid_spec=pltpu.PrefetchScalarGridSpec(
            num_scalar_prefetch=1,
            grid=(bd,),
            in_specs=[vec, vec, vec, pl.BlockSpec((None, 1, bias3.shape[2]), lambda b, pt: (b, 0, 0))]
            + [page_spec(i) for i in range(n_pages)] * 2,
            out_specs=vec,
        ),
        out_shape=jax.ShapeDtypeStruct((bd, 1, D_ATTN), BF16),
        compiler_params=_cparams(("arbitrary",)),
        name="sample_attn",
    )(page_table, q3, knew3, vnew3, bias3, *([ck_t] * n_pages), *([cv_t] * n_pages))


def _mixout_sample_kernel(x_ref, vnew_ref, unew_ref, yattn_ref, stc_ref, stp_ref, wdw_ref, bdw_ref, gcln_ref,
                          bcln_ref, wpool_ref, spool_ref, wout_ref, g_ref, lng_ref, lnb_ref, o_ref,
                          *, alpha, pos0):
    unew = unew_ref[...]
    y = bdw_ref[...] + vnew_ref[...] * wdw_ref[CONV_STATE:CONV_STATE + 1, :]
    for j in range(CONV_STATE):
        y = y + stc_ref[j] * wdw_ref[j:j + 1, :]

    wl = _pool_window_per_lane()
    win = unew
    for j in range(POOL_STATE):
        win = win + stp_ref[j] * ((POOL_STATE - j) < wl).astype(F32)
    cnt = jnp.minimum(pos0 + 1, wl).astype(F32)
    d = win / cnt - unew

    o_ref[...] = _mix_tail(x_ref[...], y, d, yattn_ref[...], gcln_ref, bcln_ref, wpool_ref, spool_ref, wout_ref,
                           g_ref, lng_ref, lnb_ref, alpha)


def _mixout_sample(x, vnew, unew, yattn, stc_t, stp_t, layer, mod, wdw, bdw, gcln, bcln, wpool, spool, wout_b, lng,
                   lnb, *, alpha, pos0):
    bd, d = x.shape
    const = lambda a: pl.BlockSpec(a.shape, lambda i: (0,) * a.ndim)
    st = lambda a: pl.BlockSpec((None,) + a.shape[1:], lambda i: (layer, 0, 0, 0))
    return pl.pallas_call(
        functools.partial(_mixout_sample_kernel, alpha=alpha, pos0=pos0),
        grid=(1,),
        in_specs=[const(x), const(vnew), const(unew), const(yattn), st(stc_t), st(stp_t), const(wdw),
                  const(bdw), const(gcln), const(bcln), const(wpool), const(spool),
                  pl.BlockSpec((None,) + wout_b.shape[1:], lambda i: (layer, 0, 0)),
                  pl.BlockSpec((None, bd, d), lambda i: (0, 0, 5)), const(lng), const(lnb)],
        out_specs=const(x),
        out_shape=jax.ShapeDtypeStruct((bd, d), F32),
        compiler_params=_cparams(("arbitrary",)),
        name="mixout_sample",
    )(x, vnew, unew, yattn, stc_t, stp_t, wdw, bdw, gcln, bcln, wpool, spool, wout_b, mod, lng, lnb)


def _block_diag(w):
    g, a, b = w.shape
    out = jnp.zeros((g * a, g * b), w.dtype)
    for i in range(g):
        out = out.at[i * a:(i + 1) * a, i * b:(i + 1) * b].set(w[i])
    return out


def _tile(n, pref):
    t = min(n, pref)
    while n % t:
        t //= 2
    return t


def kernel(x_prompt, x_sample, cache_k, cache_v, cache_kidx, state_conv, state_pool, page_table, c_prompt, c_sample,
           w_ada, b_ada, ln_g, ln_b, w_ff_in, w_ff_out, w_in, w_dw, b_dw, g_conv_ln, b_conv_ln, w_pool, s_pool,
           w_out):
    bp, s, d = x_prompt.shape
    bd, t_dec, _ = x_sample.shape
    assert t_dec == 1, "the sample path handles one decode position per sequence"
    depth = w_ada.shape[0]
    n_phys = cache_k.shape[1]
    n_pages = page_table.shape[1]
    past = n_pages * PAGE_SIZE
    f = w_ff_out.shape[2]
    alpha = (2 * depth) ** 0.25
    topk_s = min(TOPK_MAX, (past + t_dec) // 4)

    r = bp + bd
    r_pad = -(-r // 8) * 8
    c_all = jnp.concatenate([c_prompt, c_sample, jnp.zeros((r_pad - r, d), F32)], axis=0)
    mod_all = _ada(c_all, w_ada, b_ada)

    wff_in = w_ff_in.astype(BF16)
    wff_out = w_ff_out.astype(BF16)
    w_in_t = jnp.pad(jnp.transpose(w_in, (0, 2, 1)), ((0, 0), (0, D_IN_PAD - D_IN), (0, 0))).astype(BF16)
    wout_b = w_out.astype(BF16)
    wdw_pad = jnp.pad(w_dw, ((0, 0), (0, HALO - CONV_W), (0, 0)))
    ck_t = jnp.transpose(cache_k, (0, 1, 3, 4, 2)).reshape(depth, n_phys, D_ATTN, PAGE_SIZE)
    cv_t = jnp.transpose(cache_v, (0, 1, 3, 4, 2)).reshape(depth, n_phys, D_ATTN, PAGE_SIZE)
    kidx_t = jnp.transpose(cache_kidx, (0, 1, 3, 2))
    stc_t = jnp.transpose(state_conv, (0, 2, 1, 3))
    stp_t = jnp.transpose(state_pool, (0, 2, 1, 3))

    tm = _tile(s, 512)
    tf = f // 2 if (f // 2) % LANES == 0 else f
    tq = _tile(s, 128)
    tk = _tile(s, 512)

    xp = x_prompt.reshape(bp * s, d)
    xs = x_sample.reshape(bd, d)
    outs_p = [[] for _ in range(5)]
    outs_s = [[] for _ in range(5)]
    for l in range(depth):
        mod_p = mod_all[l, :bp].reshape(bp, 1, 9 * d)
        mod_s = mod_all[l, bp:r].reshape(1, bd, 9 * d)
        lng = [ln_g[l, i].reshape(1, d) for i in range(3)]
        lnb = [ln_b[l, i].reshape(1, d) for i in range(3)]
        mix_w = (wdw_pad[l], b_dw[l].reshape(1, D_CONV), g_conv_ln[l].reshape(1, D_CONV),
                 b_conv_ln[l].reshape(1, D_CONV), _block_diag(w_pool[l]).astype(BF16),
                 s_pool[l].reshape(1, D_POOL), wout_b, l)

        kw = dict(per_row=False, rows_per_seq=s, tm=tm)
        xp = _ffn(xp, mod_p, 0, wff_in, wff_out, l, 0, lng[0], lnb[0], alpha=alpha, tf=tf, **kw)
        vglu, upool, q, qi, kiwi, kt, ktb, vt, vtb, kit, kitb = _inproj(xp, mod_p, w_in_t, l, rowmajor_kv=False, **kw)
        sq = lambda a: a.reshape(bp, s, a.shape[-1])
        yattn = _attn_prompt(sq(qi), sq(kiwi), kitb, sq(q), ktb, vtb, tq=tq, tk=tk)
        xp = _mixout_prompt(sq(xp), sq(vglu), sq(upool), yattn, mod_p, *mix_w, lng[1], lnb[1],
                            tt=tm, alpha=alpha).reshape(bp * s, d)
        xp = _ffn(xp, mod_p, 6, wff_in, wff_out, l, 1, lng[2], lnb[2], alpha=alpha, tf=tf, **kw)
        outs_p[0].append(kt)
        outs_p[1].append(vt)
        outs_p[2].append(kit)
        outs_p[3].append(sq(vglu)[:, s - CONV_STATE:])
        outs_p[4].append(sq(upool)[:, s - POOL_STATE:])

        kw = dict(per_row=True, rows_per_seq=1, tm=bd)
        xs = _ffn(xs, mod_s, 0, wff_in, wff_out, l, 0, lng[0], lnb[0], alpha=alpha, tf=tf, **kw)
        vglu, upool, q, qi, kiwi, kt, _, vt, _, kit, _, k_rm, v_rm = _inproj(xs, mod_s, w_in_t, l,
                                                                             rowmajor_kv=True, **kw)
        scores = _sample_scores(page_table, qi.reshape(bd, N_IDX_HEADS, D_IDX),
                                kiwi[:, D_IDX:D_IDX + N_IDX_HEADS].reshape(bd, N_IDX_HEADS, 1),
                                kiwi[:, :D_IDX].reshape(bd, 1, D_IDX), kidx_t, l)
        bias = _sample_select(scores.reshape(bd, (n_pages + 1) * PAGE_SIZE), topk_s)
        yattn = _sample_attn(page_table, q.astype(F32).reshape(bd, 1, D_ATTN), k_rm.reshape(bd, 1, D_ATTN),
                             v_rm.reshape(bd, 1, D_ATTN), bias.reshape(bd, 1, (n_pages + 1) * PAGE_SIZE),
                             ck_t, cv_t, l).reshape(bd, D_ATTN)
        xs = _mixout_sample(xs, vglu, upool, yattn, stc_t, stp_t, l, mod_s, *mix_w[:-1], lng[1], lnb[1],
                            alpha=alpha, pos0=past)
        xs = _ffn(xs, mod_s, 6, wff_in, wff_out, l, 1, lng[2], lnb[2], alpha=alpha, tf=tf, **kw)
        outs_s[0].append(kt[0])
        outs_s[1].append(vt[0])
        outs_s[2].append(kit[0])
        outs_s[3].append(vglu)
        outs_s[4].append(upool)

    k_p, v_p = [jnp.transpose(jnp.stack(o).reshape(depth, bp, N_HEADS, HEAD_DIM, s), (0, 1, 4, 2, 3))
                for o in outs_p[:2]]
    kidx_p = jnp.transpose(jnp.stack(outs_p[2]), (0, 1, 3, 2))
    k_s, v_s = [jnp.transpose(jnp.stack(o).reshape(depth, N_HEADS, HEAD_DIM, bd), (0, 3, 1, 2))
                .reshape(depth, bd, 1, N_HEADS, HEAD_DIM) for o in outs_s[:2]]
    kidx_s = jnp.transpose(jnp.stack(outs_s[2]), (0, 2, 1)).reshape(depth, bd, 1, D_IDX)
    conv_s = jnp.transpose(jnp.concatenate([stc_t[:, 1:], jnp.stack(outs_s[3])[:, None]], axis=1), (0, 2, 1, 3))
    pool_s = jnp.transpose(jnp.concatenate([stp_t[:, 1:], jnp.stack(outs_s[4])[:, None]], axis=1), (0, 2, 1, 3))
    return (xp.reshape(bp, s, d), xs.reshape(bd, 1, d), k_p, v_p, kidx_p, jnp.stack(outs_p[3]),
            jnp.stack(outs_p[4]), k_s, v_s, kidx_s, conv_s, pool_s)
```

```python
import functools

import jax
import jax.numpy as jnp
from jax import lax
from jax.experimental import pallas as pl
from jax.experimental.pallas import tpu as pltpu

F32 = jnp.float32
BF16 = jnp.bfloat16
I32 = jnp.int32

D_CONV = 256
CONV_W = 31
CONV_STATE = CONV_W - 1
D_POOL = 256
POOL_WINDOWS = (2, 4, 8, 16)
POOL_GW = D_POOL // len(POOL_WINDOWS)
POOL_STATE = max(POOL_WINDOWS) - 1
N_HEADS = 8
HEAD_DIM = 64
D_ATTN = N_HEADS * HEAD_DIM
N_IDX_HEADS = 8
D_IDX = 64
D_QI = N_IDX_HEADS * D_IDX
TOPK_MAX = 256
PAGE_SIZE = 128
LN_EPS = 1e-5
D_IN = 2 * D_CONV + D_POOL + 3 * D_ATTN + D_QI + D_IDX + N_IDX_HEADS

LANES = 128
ROW_Q = 2 * D_CONV + D_POOL
ROW_K = ROW_Q + D_ATTN
ROW_V = ROW_K + D_ATTN
ROW_QI = ROW_V + D_ATTN
ROW_KI = ROW_QI + D_QI
D_IN_PAD = ROW_KI + LANES
HALO = 32
INT_MIN = -2 ** 31
INT_MAX = 2 ** 31 - 1
MASK_BIAS = -1e30
VMEM_LIMIT = 56 * 1024 * 1024

LOG2E = 1.4426950408889634
Q_SCALE = (HEAD_DIM ** -0.5) * LOG2E
ALIBI_SLOPES_LOG2 = tuple(LOG2E * 2.0 ** (-8.0 * (h + 1) / N_HEADS) for h in range(N_HEADS))
IDX_SCALE = (D_IDX ** -0.5) * (N_IDX_HEADS ** -0.5)


def _layer_norm(y, g, b):
    mu = jnp.mean(y, axis=-1, keepdims=True)
    d = y - mu
    var = jnp.mean(d * d, axis=-1, keepdims=True)
    return d * lax.rsqrt(var + LN_EPS) * g + b


def _silu(x):
    return x * jax.nn.sigmoid(x)


def _dot(a, b):
    return jnp.dot(a, b, preferred_element_type=F32)


def _dot_nt(a, b):
    return lax.dot_general(a, b, (((1,), (1,)), ((), ())), preferred_element_type=F32)


def _cparams(sem):
    return pltpu.CompilerParams(dimension_semantics=sem, vmem_limit_bytes=VMEM_LIMIT)


def _ada_kernel(c_ref, w_ref, b_ref, o_ref):
    s = _silu(c_ref[...]).astype(BF16)
    o_ref[...] = _dot(s, w_ref[...].astype(BF16)) + b_ref[...]


def _ada(c_all, w_ada, b_ada):
    depth, d, nd = w_ada.shape
    r = c_all.shape[0]
    return pl.pallas_call(
        _ada_kernel,
        grid=(depth, nd // d),
        in_specs=[
            pl.BlockSpec((r, d), lambda l, j: (0, 0)),
            pl.BlockSpec((None, d, d), lambda l, j: (l, 0, j)),
            pl.BlockSpec((None, 1, d), lambda l, j: (l, 0, j)),
        ],
        out_specs=pl.BlockSpec((None, r, d), lambda l, j: (l, 0, j)),
        out_shape=jax.ShapeDtypeStruct((depth, r, nd), F32),
        compiler_params=_cparams(("arbitrary", "arbitrary")),
        name="ada",
    )(c_all, w_ada, b_ada.reshape(depth, 1, nd))


def _ffn_kernel(x_ref, sh_ref, sc_ref, g_ref, wa_ref, wg_ref, wo_ref, lng_ref, lnb_ref, o_ref,
                acc_ref, h_ref, *, alpha):
    j = pl.program_id(1)

    @pl.when(j == 0)
    def _():
        h_ref[...] = (x_ref[...] * (1.0 + sc_ref[...]) + sh_ref[...]).astype(BF16)
        acc_ref[...] = jnp.zeros_like(acc_ref)

    h = h_ref[...]
    a = _dot(h, wa_ref[...])
    g = _dot(h, wg_ref[...])
    u = (_silu(g) * a).astype(BF16)
    acc_ref[...] += _dot(u, wo_ref[...])

    @pl.when(j == pl.num_programs(1) - 1)
    def _():
        y = alpha * x_ref[...] + 0.5 * g_ref[...] * acc_ref[...]
        o_ref[...] = _layer_norm(y, lng_ref[...], lnb_ref[...])


def _mod_spec(per_row, rows_per_seq, tm, d, col):
    if per_row:
        return pl.BlockSpec((None, tm, d), lambda i, *_: (0, i, col))
    return pl.BlockSpec((None, 1, d), lambda i, *_: ((i * tm) // rows_per_seq, 0, col))


def _ffn(x, mod, col0, wff_in, wff_out, layer, which, lng, lnb, *, alpha, per_row, rows_per_seq, tm, tf):
    n, d = x.shape
    f = wff_out.shape[2]
    nf = f // tf
    ms = functools.partial(_mod_spec, per_row, rows_per_seq, tm, d)
    return pl.pallas_call(
        functools.partial(_ffn_kernel, alpha=alpha),
        grid=(n // tm, nf),
        in_specs=[
            pl.BlockSpec((tm, d), lambda i, j: (i, 0)),
            ms(col0), ms(col0 + 1), ms(col0 + 2),
            pl.BlockSpec((None, None, d, tf), lambda i, j: (layer, which, 0, j)),
            pl.BlockSpec((None, None, d, tf), lambda i, j: (layer, which, 0, j + nf)),
            pl.BlockSpec((None, None, tf, d), lambda i, j: (layer, which, j, 0)),
            pl.BlockSpec((1, d), lambda i, j: (0, 0)),
            pl.BlockSpec((1, d), lambda i, j: (0, 0)),
        ],
        out_specs=pl.BlockSpec((tm, d), lambda i, j: (i, 0)),
        out_shape=jax.ShapeDtypeStruct((n, d), F32),
        scratch_shapes=[pltpu.VMEM((tm, d), F32), pltpu.VMEM((tm, d), BF16)],
        compiler_params=_cparams(("arbitrary", "arbitrary")),
        name="ffn",
    )(x, mod, mod, mod, wff_in, wff_in, wff_out, lng, lnb)


def _inproj_kernel(x_ref, sh_ref, sc_ref, w_ref, *out_refs, prompt):
    h = (x_ref[...] * (1.0 + sc_ref[...]) + sh_ref[...]).astype(BF16)
    rm = lambda r0, r1: _dot_nt(h, w_ref[r0:r1, :])
    fm = lambda r0, r1: _dot_nt(w_ref[r0:r1, :], h)
    if prompt:
        vglu_ref, upool_ref, kiwib_ref, kb_ref, qt_ref, qit_ref, wit_ref, kt_ref, vt_ref, vtb_ref, kit_ref = out_refs
    else:
        vglu_ref, upool_ref, q_ref, qi_ref, kiwi_ref, kt_ref, vt_ref, kit_ref, k_ref, v_ref = out_refs
    y = rm(0, ROW_Q)
    vglu_ref[...] = y[:, 0:D_CONV] * jax.nn.sigmoid(y[:, D_CONV:2 * D_CONV])
    upool_ref[...] = y[:, 2 * D_CONV:ROW_Q]
    kt_ref[...] = fm(ROW_K, ROW_V)
    vt = fm(ROW_V, ROW_QI)
    vt_ref[...] = vt
    kit_ref[...] = fm(ROW_KI, ROW_KI + D_IDX)
    kiwi = rm(ROW_KI, D_IN_PAD)
    if prompt:
        vtb_ref[...] = vt.astype(BF16)
        kiwib_ref[...] = kiwi.astype(BF16)
        kb_ref[...] = rm(ROW_K, ROW_V).astype(BF16)
        qt_ref[...] = (fm(ROW_Q, ROW_K) * Q_SCALE).astype(BF16)
        qit_ref[...] = fm(ROW_QI, ROW_KI).astype(BF16)
        wit = fm(ROW_KI + D_IDX, ROW_KI + D_IDX + 2 * N_IDX_HEADS)
        wit_ref[...] = wit[0:N_IDX_HEADS, :] * IDX_SCALE
    else:
        q_ref[...] = (rm(ROW_Q, ROW_K) * Q_SCALE).astype(BF16)
        qi_ref[...] = rm(ROW_QI, ROW_KI).astype(BF16)
        kiwi_ref[...] = kiwi
        k_ref[...] = rm(ROW_K, ROW_V)
        v_ref[...] = rm(ROW_V, ROW_QI)


def _inproj(x, mod, w_in_t, layer, *, per_row, rows_per_seq, tm, prompt):
    n, d = x.shape
    g = max(n // rows_per_seq, 1) if not per_row else 1
    rows = n // g
    tpb = rows // tm
    ms = functools.partial(_mod_spec, per_row, rows_per_seq, tm, d)
    row = lambda w: pl.BlockSpec((tm, w), lambda i: (i, 0))
    shp = lambda w, dt: jax.ShapeDtypeStruct((n, w), dt)
    tr = lambda w: pl.BlockSpec((None, w, tm), lambda i: (i // tpb, 0, i % tpb))
    tshp = lambda w, dt: jax.ShapeDtypeStruct((g, w, rows), dt)
    if prompt:
        outs = [(row, D_CONV, F32), (row, D_POOL, F32), (row, LANES, BF16), (row, D_ATTN, BF16),
                (tr, D_ATTN, BF16), (tr, D_QI, BF16), (tr, N_IDX_HEADS, F32), (tr, D_ATTN, F32), (tr, D_ATTN, F32),
                (tr, D_ATTN, BF16), (tr, D_IDX, F32)]
    else:
        outs = [(row, D_CONV, F32), (row, D_POOL, F32), (row, D_ATTN, BF16), (row, D_QI, BF16), (row, LANES, F32),
                (tr, D_ATTN, F32), (tr, D_ATTN, F32), (tr, D_IDX, F32), (row, D_ATTN, F32), (row, D_ATTN, F32)]
    out_specs = [kind(w) for kind, w, _ in outs]
    out_shape = [(shp if kind is row else tshp)(w, dt) for kind, w, dt in outs]
    return pl.pallas_call(
        functools.partial(_inproj_kernel, prompt=prompt),
        grid=(n // tm,),
        in_specs=[row(d), ms(3), ms(4), pl.BlockSpec((None, D_IN_PAD, d), lambda i: (layer, 0, 0))],
        out_specs=out_specs,
        out_shape=out_shape,
        compiler_params=_cparams(("arbitrary",)),
        name="inproj",
    )(x, mod, mod, w_in_t)


def _score_to_key(score, keep):
    bits = pltpu.bitcast(jnp.where(score == 0.0, 0.0, score), I32)
    key = jnp.where(bits < 0, bits ^ INT_MAX, bits)
    return jnp.where(keep, key, INT_MIN)


I16 = jnp.int16
I16_MIN = -2 ** 15


def _key_chunk(ks, tk, axis):
    return (slice(None), pl.ds(ks, tk)) if axis == 1 else (pl.ds(ks, tk), slice(None))


def _fold_tiles(x, axis, tile, op):
    parts = [lax.slice_in_dim(x, i, i + tile, axis=axis) for i in range(0, x.shape[axis], tile)]
    while len(parts) > 1:
        parts = [op(a, b) for a, b in zip(parts[0::2], parts[1::2])] + parts[len(parts) & ~1:]
    return parts[0]


def _acc_shape(n_problems, axis, dtype):
    sublane_tile = 32 // jnp.dtype(dtype).itemsize
    return (n_problems, LANES) if axis == 1 else (sublane_tile, n_problems)


def _count16(k16_ref, cand, nchunks, tk, axis):
    shape = _acc_shape(k16_ref.shape[1 - axis], axis, I16)
    cand_b = jnp.broadcast_to(cand, shape).astype(I16)
    one, zero = jnp.int16(1), jnp.int16(0)

    def body(c, acc):
        kk = k16_ref[_key_chunk(pl.multiple_of(c * tk, tk), tk, axis)]
        hits = [jnp.where(lax.slice_in_dim(kk, i, i + shape[axis], axis=axis) >= cand_b, one, zero)
                for i in range(0, tk, shape[axis])]
        while len(hits) > 1:
            hits = [a + b for a, b in zip(hits[0::2], hits[1::2])] + hits[len(hits) & ~1:]
        return acc + hits[0]

    acc = lax.fori_loop(0, nchunks, body, jnp.zeros(shape, I16))
    return jnp.sum(acc.astype(I32), axis=axis, keepdims=True)


def _search16(k16_ref, nchunks, tk, target, count_all, axis):
    def bit_body(i, carry):
        y, c_ge, c_gt = carry
        cand = y + lax.shift_left(jnp.int32(1), 15 - i)
        cnt = _count16(k16_ref, cand, nchunks, tk, axis)
        ok = cnt >= target
        return jnp.where(ok, cand, y), jnp.where(ok, cnt, c_ge), jnp.where(ok, c_gt, cnt)

    return lax.fori_loop(0, 16, bit_body,
                         (jnp.full(count_all.shape, I16_MIN, I32), count_all, jnp.zeros(count_all.shape, I32)))


def _select_threshold(key_ref, k16_ref, j_ref, nchunks, tk, topk, index_bits, axis):
    n_problems = key_ref.shape[1 - axis]
    vec = (n_problems, 1) if axis == 1 else (1, n_problems)
    chunk_shape = (n_problems, tk) if axis == 1 else (tk, n_problems)

    def for_chunks(fn):
        def body(c, carry):
            idx = _key_chunk(pl.multiple_of(c * tk, tk), tk, axis)
            fn(idx, key_ref[idx])
            return carry
        lax.fori_loop(0, nchunks, body, 0)

    def store_high(idx, kk):
        k16_ref[idx] = lax.shift_right_arithmetic(kk, 16).astype(I16)

    for_chunks(store_high)
    hi, c_ge_hi, n_above = _search16(k16_ref, nchunks, tk, topk, jnp.full(vec, INT_MAX, I32), axis)

    def store_low(idx, kk):
        in_bucket = lax.shift_right_arithmetic(kk, 16) == hi
        low = (kk & 0xFFFF) + I16_MIN
        k16_ref[idx] = jnp.where(in_bucket, low, I16_MIN).astype(I16)

    for_chunks(store_low)
    lo, c_ge_lo, _ = _search16(k16_ref, nchunks, tk, topk - n_above, c_ge_hi - n_above, axis)
    x = lax.shift_left(hi, 16) + (lo - I16_MIN)
    cnt_x = n_above + c_ge_lo

    acc_shape = _acc_shape(n_problems, axis, I32)

    def count(pred):
        def body(c, cnt):
            ks = pl.multiple_of(c * tk, tk)
            m = pred(key_ref[_key_chunk(ks, tk, axis)], ks).astype(I32)
            return cnt + _fold_tiles(m, axis, acc_shape[axis], jnp.add)
        cnt = lax.fori_loop(0, nchunks, body, jnp.zeros(acc_shape, I32))
        return jnp.sum(cnt, axis=axis, keepdims=True)

    excess = (cnt_x > topk) & (x > INT_MIN)
    j_ref[...] = jnp.full(j_ref.shape, INT_MAX, I32)

    @pl.when(jnp.max(excess.astype(I32)) > 0)
    def _():
        need = topk - count(lambda kk, ks: kk > x)
        pos = lax.broadcasted_iota(I32, chunk_shape, axis)

        def idx_body(i, j):
            cand = j + lax.shift_left(jnp.int32(1), index_bits - 1 - i)
            cnt = count(lambda kk, ks: (kk == x) & (pos + ks < cand))
            return jnp.where(cnt < need, cand, j)

        j = lax.fori_loop(0, index_bits, idx_body, jnp.zeros(vec, I32))
        j_ref[...] = jnp.broadcast_to(jnp.where(excess, j, INT_MAX), j_ref.shape)

    return jnp.maximum(x, INT_MIN + 1), lax.slice_in_dim(j_ref[...], 0, 1, axis=axis)


def _selection_bias(kk, col, x, jmax):
    sel = (kk > x) | ((kk == x) & (col <= jmax))
    return jnp.where(sel, 0.0, MASK_BIAS)


def _attn_kernel(qit_ref, wit_ref, kiwib_ref, qt_ref, kb_ref, vt_ref, o_ref, key_ref, k16_ref, j_ref, s_ref,
                 *, tq, tk, topk, index_bits):
    t0 = pl.program_id(1) * tq
    nkc = (t0 + tq + tk - 1) // tk
    wit = wit_ref[...]
    heads = [slice(h * HEAD_DIM, (h + 1) * HEAD_DIM) for h in range(N_HEADS)]
    kpos = lax.broadcasted_iota(I32, (tk, tq), 0)
    qpos = lax.broadcasted_iota(I32, (tk, tq), 1) + t0

    def score_body(c, carry):
        ks = pl.multiple_of(c * tk, tk)
        ki = kiwib_ref[pl.ds(ks, tk), 0:D_IDX]
        acc = jnp.zeros((tk, tq), F32)
        for h in range(N_IDX_HEADS):
            s = _dot(ki, qit_ref[h * D_IDX:(h + 1) * D_IDX, :])
            acc = acc + jnp.maximum(s, 0.0) * wit[h:h + 1, :]
        key_ref[pl.ds(ks, tk), :] = _score_to_key(acc, kpos + ks <= qpos)
        return carry

    lax.fori_loop(0, nkc, score_body, 0)
    x, jmax = _select_threshold(key_ref, k16_ref, j_ref, nkc, tk, topk, index_bits, axis=0)

    tmax = (t0 + tq - 1).astype(F32)
    kposf = kpos.astype(F32)

    def attend_body(c, carry):
        ms, ls, accs = carry
        ks = pl.multiple_of(c * tk, tk)
        bias_c = _selection_bias(key_ref[pl.ds(ks, tk), :], kpos + ks, x, jmax)
        rel = kposf + (ks.astype(F32) - tmax)
        new_m = []
        for h in range(N_HEADS):
            s = (_dot(kb_ref[pl.ds(ks, tk), heads[h]], qt_ref[heads[h], :]) + bias_c
                 + ALIBI_SLOPES_LOG2[h] * rel)
            s_ref[h] = s
            new_m.append(jnp.maximum(ms[h], jnp.max(s, axis=0, keepdims=True)))
        new_l, new_a = [], []
        for h in range(N_HEADS):
            a = jnp.exp2(ms[h] - new_m[h])
            p = jnp.exp2(s_ref[h] - new_m[h])
            new_l.append(a * ls[h] + jnp.sum(p, axis=0, keepdims=True))
            new_a.append(a * accs[h] + _dot(vt_ref[heads[h], pl.ds(ks, tk)], p.astype(BF16)))
        return tuple(new_m), tuple(new_l), tuple(new_a)

    init = (tuple(jnp.full((1, tq), MASK_BIAS, F32) for _ in range(N_HEADS)),
            tuple(jnp.zeros((1, tq), F32) for _ in range(N_HEADS)),
            tuple(jnp.zeros((HEAD_DIM, tq), F32) for _ in range(N_HEADS)))
    _, ls, accs = lax.fori_loop(0, nkc, attend_body, init)
    out_t = jnp.concatenate([accs[h] / ls[h] for h in range(N_HEADS)], axis=0)
    o_ref[...] = out_t.T.astype(o_ref.dtype)


def _attn_prompt(qit, wit, kiwib, qt, kb, vtb, *, tq, tk):
    b, s, _ = kb.shape
    topk = min(TOPK_MAX, s // 4)
    blk_t = lambda w: pl.BlockSpec((None, w, tq), lambda bi, i: (bi, 0, i))
    full = lambda w: pl.BlockSpec((None, s, w), lambda bi, i: (bi, 0, 0))
    return pl.pallas_call(
        functools.partial(_attn_kernel, tq=tq, tk=tk, topk=topk, index_bits=max(s - 1, 1).bit_length()),
        grid=(b, s // tq),
        in_specs=[blk_t(D_QI), blk_t(N_IDX_HEADS), full(LANES), blk_t(D_ATTN), full(D_ATTN),
                  pl.BlockSpec((None, D_ATTN, s), lambda bi, i: (bi, 0, 0))],
        out_specs=pl.BlockSpec((None, tq, D_ATTN), lambda bi, i: (bi, i, 0)),
        out_shape=jax.ShapeDtypeStruct((b, s, D_ATTN), BF16),
        scratch_shapes=[pltpu.VMEM((s, tq), I32), pltpu.VMEM((s, tq), I16),
                        pltpu.VMEM(_acc_shape(tq, 0, I32), I32), pltpu.VMEM((N_HEADS, tk, tq), F32)],
        compiler_params=_cparams(("arbitrary", "arbitrary")),
        name="attn_prompt",
    )(qit, wit, kiwib, qt, kb, vtb)


def _pool_window_per_lane():
    lane = lax.broadcasted_iota(I32, (1, D_POOL), 1)
    return lax.shift_left(jnp.int32(POOL_WINDOWS[0]), lane // POOL_GW)


def _mix_tail(x, yconv_pre, d_pool, yattn, gcln_ref, bcln_ref, wpool_ref, spool_ref, wout_ref, g_ref, lng_ref,
              lnb_ref, alpha):
    yconv = _silu(_layer_norm(yconv_pre, gcln_ref[...], bcln_ref[...]))
    ypool = _dot(d_pool.astype(BF16), wpool_ref[...]) * spool_ref[...]
    cat = jnp.concatenate([yconv.astype(BF16), ypool.astype(BF16), yattn], axis=1)
    y = _dot(cat, wout_ref[...])
    return _layer_norm(alpha * x + g_ref[...] * y, lng_ref[...], lnb_ref[...])


def _mixout_kernel(x_ref, vglu_ref, upool_ref, yattn_ref, wdw_ref, bdw_ref, gcln_ref, bcln_ref, wpool_ref,
                   spool_ref, wout_ref, g_ref, lng_ref, lnb_ref, o_ref, extc_ref, extp_ref, *, tt, alpha):
    ti = pl.program_id(1)
    t0 = pl.multiple_of(ti * tt, tt)
    hs = pl.multiple_of(jnp.maximum(t0 - HALO, 0), 8)
    first = ti == 0
    extc_ref[0:HALO, :] = jnp.where(first, 0.0, vglu_ref[pl.ds(hs, HALO), :])
    extc_ref[HALO:, :] = vglu_ref[pl.ds(t0, tt), :]
    extp_ref[0:HALO, :] = jnp.where(first, 0.0, upool_ref[pl.ds(hs, HALO), :])
    u = upool_ref[pl.ds(t0, tt), :]
    extp_ref[HALO:, :] = u

    y = jnp.broadcast_to(bdw_ref[...], (tt, D_CONV))
    for j in range(CONV_W):
        y = y + extc_ref[pl.ds(HALO - CONV_STATE + j, tt), :] * wdw_ref[j:j + 1, :]

    wl = _pool_window_per_lane()
    win = u
    for i in range(1, max(POOL_WINDOWS)):
        win = win + extp_ref[pl.ds(HALO - i, tt), :] * (i < wl).astype(F32)
    pos = lax.broadcasted_iota(I32, (tt, 1), 0) + t0
    cnt = jnp.minimum(pos + 1, wl).astype(F32)
    d = win / cnt - u

    o_ref[...] = _mix_tail(x_ref[...], y, d, yattn_ref[...], gcln_ref, bcln_ref, wpool_ref, spool_ref, wout_ref,
                           g_ref, lng_ref, lnb_ref, alpha)


def _mixout_prompt(x, vglu, upool, yattn, mod, wdw, bdw, gcln, bcln, wpool, spool, wout_b, layer, lng, lnb,
                   *, tt, alpha):
    b, s, d = x.shape
    blk = lambda w: pl.BlockSpec((None, tt, w), lambda bi, i: (bi, i, 0))
    full = lambda w: pl.BlockSpec((None, s, w), lambda bi, i: (bi, 0, 0))
    const = lambda a: pl.BlockSpec(a.shape, lambda bi, i: (0,) * a.ndim)
    return pl.pallas_call(
        functools.partial(_mixout_kernel, tt=tt, alpha=alpha),
        grid=(b, s // tt),
        in_specs=[blk(d), full(D_CONV), full(D_POOL), blk(D_ATTN), const(wdw), const(bdw), const(gcln),
                  const(bcln), const(wpool), const(spool),
                  pl.BlockSpec((None,) + wout_b.shape[1:], lambda bi, i: (layer, 0, 0)),
                  pl.BlockSpec((None, 1, d), lambda bi, i: (bi, 0, 5)), const(lng), const(lnb)],
        out_specs=blk(d),
        out_shape=jax.ShapeDtypeStruct((b, s, d), F32),
        scratch_shapes=[pltpu.VMEM((HALO + tt, D_CONV), F32), pltpu.VMEM((HALO + tt, D_POOL), F32)],
        compiler_params=_cparams(("arbitrary", "arbitrary")),
        name="mixout_prompt",
    )(x, vglu, upool, yattn, wdw, bdw, gcln, bcln, wpool, spool, wout_b, mod, lng, lnb)


def _sample_score_kernel(pt_ref, qi_ref, wi_ref, kinew_ref, *rest, n_pages):
    page_refs, o_ref = rest[:n_pages], rest[n_pages]
    qi = qi_ref[...]
    wi = wi_ref[...] * IDX_SCALE
    for p in range(n_pages):
        s = _dot(qi, page_refs[p][...].astype(BF16))
        o_ref[p:p + 1, :] = jnp.sum(jnp.maximum(s, 0.0) * wi, axis=0, keepdims=True)
    kn = kinew_ref[...].astype(BF16).astype(F32)
    sn = jnp.sum(qi.astype(F32) * kn, axis=1, keepdims=True)
    snew = jnp.sum(jnp.maximum(sn, 0.0) * wi, axis=0, keepdims=True)
    lane = lax.broadcasted_iota(I32, (1, PAGE_SIZE), 1)
    o_ref[n_pages:n_pages + 1, :] = jnp.where(lane == 0, snew, -jnp.inf)


def _sample_scores(page_table, qi3, wi3, kinew3, kidx_t, layer):
    bd, n_pages = page_table.shape
    page_spec = lambda p: pl.BlockSpec((None, None, D_IDX, PAGE_SIZE), lambda b, pt: (layer, pt[b, p], 0, 0))
    return pl.pallas_call(
        functools.partial(_sample_score_kernel, n_pages=n_pages),
        grid_spec=pltpu.PrefetchScalarGridSpec(
            num_scalar_prefetch=1,
            grid=(bd,),
            in_specs=[pl.BlockSpec((None, N_IDX_HEADS, D_IDX), lambda b, pt: (b, 0, 0)),
                      pl.BlockSpec((None, N_IDX_HEADS, 1), lambda b, pt: (b, 0, 0)),
                      pl.BlockSpec((None, 1, D_IDX), lambda b, pt: (b, 0, 0))]
            + [page_spec(p) for p in range(n_pages)],
            out_specs=pl.BlockSpec((None, n_pages + 1, PAGE_SIZE), lambda b, pt: (b, 0, 0)),
        ),
        out_shape=jax.ShapeDtypeStruct((bd, n_pages + 1, PAGE_SIZE), F32),
        compiler_params=_cparams(("arbitrary",)),
        name="sample_scores",
    )(page_table, qi3, wi3, kinew3, *([kidx_t] * n_pages))


def _sample_select_kernel(s_ref, bias_ref, key_ref, k16_ref, j_ref, *, topk, index_bits):
    s = s_ref[...]
    key_ref[...] = _score_to_key(s, s > -jnp.inf)
    x, jmax = _select_threshold(key_ref, k16_ref, j_ref, s.shape[1] // LANES, LANES, topk, index_bits, axis=1)
    col = lax.broadcasted_iota(I32, s.shape, 1)
    bias_ref[...] = _selection_bias(key_ref[...], col, x, jmax)


def _sample_select(scores2d, topk):
    bd, n = scores2d.shape
    return pl.pallas_call(
        functools.partial(_sample_select_kernel, topk=topk, index_bits=max(n - 1, 1).bit_length()),
        out_shape=jax.ShapeDtypeStruct((bd, n), F32),
        scratch_shapes=[pltpu.VMEM((bd, n), I32), pltpu.VMEM((bd, n), I16), pltpu.VMEM((bd, LANES), I32)],
        compiler_params=pltpu.CompilerParams(vmem_limit_bytes=VMEM_LIMIT),
        name="sample_select",
    )(scores2d)


def _sample_attn_kernel(pt_ref, q_ref, knew_ref, vnew_ref, bias_ref, *rest, n_pages):
    kt_refs, vt_refs, o_ref = rest[:n_pages], rest[n_pages:2 * n_pages], rest[2 * n_pages]
    past = n_pages * PAGE_SIZE

    hrow = lax.broadcasted_iota(I32, (N_HEADS, D_ATTN), 0)
    hcol = lax.broadcasted_iota(I32, (N_HEADS, D_ATTN), 1) // HEAD_DIM
    own = hrow == hcol
    qbd = jnp.where(own, jnp.broadcast_to(q_ref[...], (N_HEADS, D_ATTN)), 0.0)
    qbd_b = qbd.astype(BF16)
    slopes = LOG2E * jnp.exp2(-8.0 * (lax.broadcasted_iota(I32, (N_HEADS, 1), 0) + 1).astype(F32) / N_HEADS)

    s = jnp.concatenate([_dot(qbd_b, kt_refs[i][...].astype(BF16)) for i in range(n_pages)], axis=1)
    dist = (past - lax.broadcasted_iota(I32, (1, past), 1)).astype(F32)
    s = s - slopes * dist + bias_ref[:, 0:past]
    sn = jnp.sum(qbd * knew_ref[...], axis=1, keepdims=True) + bias_ref[:, past:past + 1]

    m = jnp.maximum(jnp.max(s, axis=1, keepdims=True), sn)
    p = jnp.exp2(s - m)
    pn = jnp.exp2(sn - m)
    l = jnp.sum(p, axis=1, keepdims=True) + pn
    pb = p.astype(BF16)
    acc = pn * vnew_ref[...]
    for i in range(n_pages):
        acc = acc + _dot_nt(pb[:, i * PAGE_SIZE:(i + 1) * PAGE_SIZE], vt_refs[i][...].astype(BF16))
    out = jnp.where(own, acc / l, 0.0)
    o_ref[...] = jnp.sum(out, axis=0, keepdims=True).astype(o_ref.dtype)


def _sample_attn(page_table, q3, knew3, vnew3, bias3, ck_t, cv_t, layer):
    bd, n_pages = page_table.shape
    page_spec = lambda i: pl.BlockSpec((None, None, D_ATTN, PAGE_SIZE), lambda b, pt: (layer, pt[b, i], 0, 0))
    vec = pl.BlockSpec((None, 1, D_ATTN), lambda b, pt: (b, 0, 0))
    return pl.pallas_call(
        functools.partial(_sample_attn_kernel, n_pages=n_pages),
        grid_spec=pltpu.PrefetchScalarGridSpec(
            num_scalar_prefetch=1,
            grid=(bd,),
            in_specs=[vec, vec, vec, pl.BlockSpec((None, 1, bias3.shape[2]), lambda b, pt: (b, 0, 0))]
            + [page_spec(i) for i in range(n_pages)] * 2,
            out_specs=vec,
        ),
        out_shape=jax.ShapeDtypeStruct((bd, 1, D_ATTN), BF16),
        compiler_params=_cparams(("arbitrary",)),
        name="sample_attn",
    )(page_table, q3, knew3, vnew3, bias3, *([ck_t] * n_pages), *([cv_t] * n_pages))


def _mixout_sample_kernel(x_ref, vnew_ref, unew_ref, yattn_ref, stc_ref, stp_ref, wdw_ref, bdw_ref, gcln_ref,
                          bcln_ref, wpool_ref, spool_ref, wout_ref, g_ref, lng_ref, lnb_ref, o_ref,
                          *, alpha, pos0):
    unew = unew_ref[...]
    y = bdw_ref[...] + vnew_ref[...] * wdw_ref[CONV_STATE:CONV_STATE + 1, :]
    for j in range(CONV_STATE):
        y = y + stc_ref[j] * wdw_ref[j:j + 1, :]

    wl = _pool_window_per_lane()
    win = unew
    for j in range(POOL_STATE):
        win = win + stp_ref[j] * ((POOL_STATE - j) < wl).astype(F32)
    cnt = jnp.minimum(pos0 + 1, wl).astype(F32)
    d = win / cnt - unew

    o_ref[...] = _mix_tail(x_ref[...], y, d, yattn_ref[...], gcln_ref, bcln_ref, wpool_ref, spool_ref, wout_ref,
                           g_ref, lng_ref, lnb_ref, alpha)


def _mixout_sample(x, vnew, unew, yattn, stc_t, stp_t, layer, mod, wdw, bdw, gcln, bcln, wpool, spool, wout_b, lng,
                   lnb, *, alpha, pos0):
    bd, d = x.shape
    const = lambda a: pl.BlockSpec(a.shape, lambda i: (0,) * a.ndim)
    st = lambda a: pl.BlockSpec((None,) + a.shape[1:], lambda i: (layer, 0, 0, 0))
    return pl.pallas_call(
        functools.partial(_mixout_sample_kernel, alpha=alpha, pos0=pos0),
        grid=(1,),
        in_specs=[const(x), const(vnew), const(unew), const(yattn), st(stc_t), st(stp_t), const(wdw),
                  const(bdw), const(gcln), const(bcln), const(wpool), const(spool),
                  pl.BlockSpec((None,) + wout_b.shape[1:], lambda i: (layer, 0, 0)),
                  pl.BlockSpec((None, bd, d), lambda i: (0, 0, 5)), const(lng), const(lnb)],
        out_specs=const(x),
        out_shape=jax.ShapeDtypeStruct((bd, d), F32),
        compiler_params=_cparams(("arbitrary",)),
        name="mixout_sample",
    )(x, vnew, unew, yattn, stc_t, stp_t, wdw, bdw, gcln, bcln, wpool, spool, wout_b, mod, lng, lnb)


def _block_diag(w):
    g, a, b = w.shape
    out = jnp.zeros((g * a, g * b), w.dtype)
    for i in range(g):
        out = out.at[i * a:(i + 1) * a, i * b:(i + 1) * b].set(w[i])
    return out


def _tile(n, pref):
    t = min(n, pref)
    while n % t:
        t //= 2
    return t


def kernel(x_prompt, x_sample, cache_k, cache_v, cache_kidx, state_conv, state_pool, page_table, c_prompt, c_sample,
           w_ada, b_ada, ln_g, ln_b, w_ff_in, w_ff_out, w_in, w_dw, b_dw, g_conv_ln, b_conv_ln, w_pool, s_pool,
           w_out):
    bp, s, d = x_prompt.shape
    bd, t_dec, _ = x_sample.shape
    assert t_dec == 1, "the sample path handles one decode position per sequence"
    depth = w_ada.shape[0]
    n_phys = cache_k.shape[1]
    n_pages = page_table.shape[1]
    past = n_pages * PAGE_SIZE
    f = w_ff_out.shape[2]
    alpha = (2 * depth) ** 0.25
    topk_s = min(TOPK_MAX, (past + t_dec) // 4)

    r = bp + bd
    r_pad = -(-r // 8) * 8
    c_all = jnp.concatenate([c_prompt, c_sample, jnp.zeros((r_pad - r, d), F32)], axis=0)
    mod_all = _ada(c_all, w_ada, b_ada)

    wff_in = w_ff_in.astype(BF16)
    wff_out = w_ff_out.astype(BF16)
    w_in_t = jnp.pad(jnp.transpose(w_in, (0, 2, 1)), ((0, 0), (0, D_IN_PAD - D_IN), (0, 0))).astype(BF16)
    wout_b = w_out.astype(BF16)
    wdw_pad = jnp.pad(w_dw, ((0, 0), (0, HALO - CONV_W), (0, 0)))
    ck_t = jnp.transpose(cache_k, (0, 1, 3, 4, 2)).reshape(depth, n_phys, D_ATTN, PAGE_SIZE)
    cv_t = jnp.transpose(cache_v, (0, 1, 3, 4, 2)).reshape(depth, n_phys, D_ATTN, PAGE_SIZE)
    kidx_t = jnp.transpose(cache_kidx, (0, 1, 3, 2))
    stc_t = jnp.transpose(state_conv, (0, 2, 1, 3))
    stp_t = jnp.transpose(state_pool, (0, 2, 1, 3))

    tm = _tile(s, 512)
    tf = f // 2 if (f // 2) % LANES == 0 else f
    tq = _tile(s, 256)
    tk = _tile(s, 512)

    xp = x_prompt.reshape(bp * s, d)
    xs = x_sample.reshape(bd, d)
    outs_p = [[] for _ in range(5)]
    outs_s = [[] for _ in range(5)]
    for l in range(depth):
        mod_p = mod_all[l, :bp].reshape(bp, 1, 9 * d)
        mod_s = mod_all[l, bp:r].reshape(1, bd, 9 * d)
        lng = [ln_g[l, i].reshape(1, d) for i in range(3)]
        lnb = [ln_b[l, i].reshape(1, d) for i in range(3)]
        mix_w = (wdw_pad[l], b_dw[l].reshape(1, D_CONV), g_conv_ln[l].reshape(1, D_CONV),
                 b_conv_ln[l].reshape(1, D_CONV), _block_diag(w_pool[l]).astype(BF16),
                 s_pool[l].reshape(1, D_POOL), wout_b, l)

        kw = dict(per_row=False, rows_per_seq=s, tm=tm)
        xp = _ffn(xp, mod_p, 0, wff_in, wff_out, l, 0, lng[0], lnb[0], alpha=alpha, tf=tf, **kw)
        vglu, upool, kiwib, kb, qt, qit, wit, kt, vt, vtb, kit = _inproj(xp, mod_p, w_in_t, l, prompt=True, **kw)
        sq = lambda a: a.reshape(bp, s, a.shape[-1])
        yattn = _attn_prompt(qit, wit, sq(kiwib), qt, sq(kb), vtb, tq=tq, tk=tk)
        xp = _mixout_prompt(sq(xp), sq(vglu), sq(upool), yattn, mod_p, *mix_w, lng[1], lnb[1],
                            tt=tm, alpha=alpha).reshape(bp * s, d)
        xp = _ffn(xp, mod_p, 6, wff_in, wff_out, l, 1, lng[2], lnb[2], alpha=alpha, tf=tf, **kw)
        outs_p[0].append(kt)
        outs_p[1].append(vt)
        outs_p[2].append(kit)
        outs_p[3].append(sq(vglu)[:, s - CONV_STATE:])
        outs_p[4].append(sq(upool)[:, s - POOL_STATE:])

        kw = dict(per_row=True, rows_per_seq=1, tm=bd)
        xs = _ffn(xs, mod_s, 0, wff_in, wff_out, l, 0, lng[0], lnb[0], alpha=alpha, tf=tf, **kw)
        vglu, upool, q, qi, kiwi, kt, vt, kit, k_rm, v_rm = _inproj(xs, mod_s, w_in_t, l, prompt=False, **kw)
        scores = _sample_scores(page_table, qi.reshape(bd, N_IDX_HEADS, D_IDX),
                                kiwi[:, D_IDX:D_IDX + N_IDX_HEADS].reshape(bd, N_IDX_HEADS, 1),
                                kiwi[:, :D_IDX].reshape(bd, 1, D_IDX), kidx_t, l)
        bias = _sample_select(scores.reshape(bd, (n_pages + 1) * PAGE_SIZE), topk_s)
        yattn = _sample_attn(page_table, q.astype(F32).reshape(bd, 1, D_ATTN), k_rm.reshape(bd, 1, D_ATTN),
                             v_rm.reshape(bd, 1, D_ATTN), bias.reshape(bd, 1, (n_pages + 1) * PAGE_SIZE),
                             ck_t, cv_t, l).reshape(bd, D_ATTN)
        xs = _mixout_sample(xs, vglu, upool, yattn, stc_t, stp_t, l, mod_s, *mix_w[:-1], lng[1], lnb[1],
                            alpha=alpha, pos0=past)
        xs = _ffn(xs, mod_s, 6, wff_in, wff_out, l, 1, lng[2], lnb[2], alpha=alpha, tf=tf, **kw)
        outs_s[0].append(kt[0])
        outs_s[1].append(vt[0])
        outs_s[2].append(kit[0])
        outs_s[3].append(vglu)
        outs_s[4].append(upool)

    k_p, v_p = [jnp.transpose(jnp.stack(o).reshape(depth, bp, N_HEADS, HEAD_DIM, s), (0, 1, 4, 2, 3))
                for o in outs_p[:2]]
    kidx_p = jnp.transpose(jnp.stack(outs_p[2]), (0, 1, 3, 2))
    k_s, v_s = [jnp.transpose(jnp.stack(o).reshape(depth, N_HEADS, HEAD_DIM, bd), (0, 3, 1, 2))
                .reshape(depth, bd, 1, N_HEADS, HEAD_DIM) for o in outs_s[:2]]
    kidx_s = jnp.transpose(jnp.stack(outs_s[2]), (0, 2, 1)).reshape(depth, bd, 1, D_IDX)
    conv_s = jnp.transpose(jnp.concatenate([stc_t[:, 1:], jnp.stack(outs_s[3])[:, None]], axis=1), (0, 2, 1, 3))
    pool_s = jnp.transpose(jnp.concatenate([stp_t[:, 1:], jnp.stack(outs_s[4])[:, None]], axis=1), (0, 2, 1, 3))
    return (xp.reshape(bp, s, d), xs.reshape(bd, 1, d), k_p, v_p, kidx_p, jnp.stack(outs_p[3]),
            jnp.stack(outs_p[4]), k_s, v_s, kidx_s, conv_s, pool_s)
```

```python
import functools

import jax
import jax.numpy as jnp
from jax import lax
from jax.experimental import pallas as pl
from jax.experimental.pallas import tpu as pltpu

F32 = jnp.float32
BF16 = jnp.bfloat16
I32 = jnp.int32

D_CONV = 256
CONV_W = 31
CONV_STATE = CONV_W - 1
D_POOL = 256
POOL_WINDOWS = (2, 4, 8, 16)
POOL_GW = D_POOL // len(POOL_WINDOWS)
POOL_STATE = max(POOL_WINDOWS) - 1
N_HEADS = 8
HEAD_DIM = 64
D_ATTN = N_HEADS * HEAD_DIM
N_IDX_HEADS = 8
D_IDX = 64
D_QI = N_IDX_HEADS * D_IDX
TOPK_MAX = 256
PAGE_SIZE = 128
LN_EPS = 1e-5
D_IN = 2 * D_CONV + D_POOL + 3 * D_ATTN + D_QI + D_IDX + N_IDX_HEADS

LANES = 128
ROW_Q = 2 * D_CONV + D_POOL
ROW_K = ROW_Q + D_ATTN
ROW_V = ROW_K + D_ATTN
ROW_QI = ROW_V + D_ATTN
ROW_KI = ROW_QI + D_QI
D_IN_PAD = ROW_KI + LANES
HALO = 32
INT_MIN = -2 ** 31
INT_MAX = 2 ** 31 - 1
MASK_BIAS = -1e30
VMEM_LIMIT = 56 * 1024 * 1024

LOG2E = 1.4426950408889634
Q_SCALE = (HEAD_DIM ** -0.5) * LOG2E
ALIBI_SLOPES_LOG2 = tuple(LOG2E * 2.0 ** (-8.0 * (h + 1) / N_HEADS) for h in range(N_HEADS))
IDX_SCALE = (D_IDX ** -0.5) * (N_IDX_HEADS ** -0.5)


def _layer_norm(y, g, b):
    mu = jnp.mean(y, axis=-1, keepdims=True)
    d = y - mu
    var = jnp.mean(d * d, axis=-1, keepdims=True)
    return d * lax.rsqrt(var + LN_EPS) * g + b


def _silu(x):
    return x * jax.nn.sigmoid(x)


def _dot(a, b):
    return jnp.dot(a, b, preferred_element_type=F32)


def _dot_nt(a, b):
    return lax.dot_general(a, b, (((1,), (1,)), ((), ())), preferred_element_type=F32)


def _cparams(sem):
    return pltpu.CompilerParams(dimension_semantics=sem, vmem_limit_bytes=VMEM_LIMIT)


def _ada_kernel(c_ref, w_ref, b_ref, o_ref):
    s = _silu(c_ref[...]).astype(BF16)
    o_ref[...] = _dot(s, w_ref[...].astype(BF16)) + b_ref[...]


def _ada(c_all, w_ada, b_ada):
    depth, d, nd = w_ada.shape
    r = c_all.shape[0]
    return pl.pallas_call(
        _ada_kernel,
        grid=(depth, nd // d),
        in_specs=[
            pl.BlockSpec((r, d), lambda l, j: (0, 0)),
            pl.BlockSpec((None, d, d), lambda l, j: (l, 0, j)),
            pl.BlockSpec((None, 1, d), lambda l, j: (l, 0, j)),
        ],
        out_specs=pl.BlockSpec((None, r, d), lambda l, j: (l, 0, j)),
        out_shape=jax.ShapeDtypeStruct((depth, r, nd), F32),
        compiler_params=_cparams(("arbitrary", "arbitrary")),
        name="ada",
    )(c_all, w_ada, b_ada.reshape(depth, 1, nd))


def _ffn_kernel(x_ref, sh_ref, sc_ref, g_ref, wa_ref, wg_ref, wo_ref, lng_ref, lnb_ref, o_ref, *, alpha, n_sub):
    ts = x_ref.shape[0] // n_sub
    for s in range(n_sub):
        rows = slice(s * ts, (s + 1) * ts)
        mod = lambda ref: ref[...] if ref.shape[0] == 1 else ref[rows, :]
        x = x_ref[rows, :]
        h = (x * (1.0 + mod(sc_ref)) + mod(sh_ref)).astype(BF16)
        a = _dot(h, wa_ref[...])
        g = _dot(h, wg_ref[...])
        u = (_silu(g) * a).astype(BF16)
        y = alpha * x + 0.5 * mod(g_ref) * _dot(u, wo_ref[...])
        o_ref[rows, :] = _layer_norm(y, lng_ref[...], lnb_ref[...])


def _mod_spec(per_row, rows_per_seq, tm, d, col):
    if per_row:
        return pl.BlockSpec((None, tm, d), lambda i, *_: (0, i, col))
    return pl.BlockSpec((None, 1, d), lambda i, *_: ((i * tm) // rows_per_seq, 0, col))


def _ffn(x, mod, col0, wff_in, wff_out, layer, which, lng, lnb, *, alpha, per_row, rows_per_seq, tm, n_sub):
    n, d = x.shape
    f = wff_out.shape[2]
    ms = functools.partial(_mod_spec, per_row, rows_per_seq, tm, d)
    resident = pl.Buffered(1)
    return pl.pallas_call(
        functools.partial(_ffn_kernel, alpha=alpha, n_sub=n_sub),
        grid=(n // tm,),
        in_specs=[
            pl.BlockSpec((tm, d), lambda i: (i, 0)),
            ms(col0), ms(col0 + 1), ms(col0 + 2),
            pl.BlockSpec((None, None, d, f), lambda i: (layer, which, 0, 0), pipeline_mode=resident),
            pl.BlockSpec((None, None, d, f), lambda i: (layer, which, 0, 1), pipeline_mode=resident),
            pl.BlockSpec((None, None, f, d), lambda i: (layer, which, 0, 0), pipeline_mode=resident),
            pl.BlockSpec((1, d), lambda i: (0, 0)),
            pl.BlockSpec((1, d), lambda i: (0, 0)),
        ],
        out_specs=pl.BlockSpec((tm, d), lambda i: (i, 0)),
        out_shape=jax.ShapeDtypeStruct((n, d), F32),
        compiler_params=_cparams(("arbitrary",)),
        name="ffn",
    )(x, mod, mod, mod, wff_in, wff_in, wff_out, lng, lnb)


def _inproj_kernel(x_ref, sh_ref, sc_ref, w_ref, *out_refs, prompt):
    h = (x_ref[...] * (1.0 + sc_ref[...]) + sh_ref[...]).astype(BF16)
    rm = lambda r0, r1: _dot_nt(h, w_ref[r0:r1, :])
    fm = lambda r0, r1: _dot_nt(w_ref[r0:r1, :], h)
    if prompt:
        vglu_ref, upool_ref, kiwib_ref, kb_ref, qt_ref, qit_ref, wit_ref, kt_ref, vt_ref, vtb_ref, kit_ref = out_refs
    else:
        vglu_ref, upool_ref, q_ref, qi_ref, kiwi_ref, kt_ref, vt_ref, kit_ref, k_ref, v_ref = out_refs
    y = rm(0, ROW_Q)
    vglu_ref[...] = y[:, 0:D_CONV] * jax.nn.sigmoid(y[:, D_CONV:2 * D_CONV])
    upool_ref[...] = y[:, 2 * D_CONV:ROW_Q]
    kt_ref[...] = fm(ROW_K, ROW_V)
    vt = fm(ROW_V, ROW_QI)
    vt_ref[...] = vt
    kit_ref[...] = fm(ROW_KI, ROW_KI + D_IDX)
    kiwi = rm(ROW_KI, D_IN_PAD)
    if prompt:
        vtb_ref[...] = vt.astype(BF16)
        kiwib_ref[...] = kiwi.astype(BF16)
        kb_ref[...] = rm(ROW_K, ROW_V).astype(BF16)
        qt_ref[...] = (fm(ROW_Q, ROW_K) * Q_SCALE).astype(BF16)
        qit_ref[...] = fm(ROW_QI, ROW_KI).astype(BF16)
        wit = fm(ROW_KI + D_IDX, ROW_KI + D_IDX + 2 * N_IDX_HEADS)
        wit_ref[...] = wit[0:N_IDX_HEADS, :] * IDX_SCALE
    else:
        q_ref[...] = (rm(ROW_Q, ROW_K) * Q_SCALE).astype(BF16)
        qi_ref[...] = rm(ROW_QI, ROW_KI).astype(BF16)
        kiwi_ref[...] = kiwi
        k_ref[...] = rm(ROW_K, ROW_V)
        v_ref[...] = rm(ROW_V, ROW_QI)


def _inproj(x, mod, w_in_t, layer, *, per_row, rows_per_seq, tm, prompt):
    n, d = x.shape
    g = max(n // rows_per_seq, 1) if not per_row else 1
    rows = n // g
    tpb = rows // tm
    ms = functools.partial(_mod_spec, per_row, rows_per_seq, tm, d)
    row = lambda w: pl.BlockSpec((tm, w), lambda i: (i, 0))
    shp = lambda w, dt: jax.ShapeDtypeStruct((n, w), dt)
    tr = lambda w: pl.BlockSpec((None, w, tm), lambda i: (i // tpb, 0, i % tpb))
    tshp = lambda w, dt: jax.ShapeDtypeStruct((g, w, rows), dt)
    if prompt:
        outs = [(row, D_CONV, F32), (row, D_POOL, F32), (row, LANES, BF16), (row, D_ATTN, BF16),
                (tr, D_ATTN, BF16), (tr, D_QI, BF16), (tr, N_IDX_HEADS, F32), (tr, D_ATTN, F32), (tr, D_ATTN, F32),
                (tr, D_ATTN, BF16), (tr, D_IDX, F32)]
    else:
        outs = [(row, D_CONV, F32), (row, D_POOL, F32), (row, D_ATTN, BF16), (row, D_QI, BF16), (row, LANES, F32),
                (tr, D_ATTN, F32), (tr, D_ATTN, F32), (tr, D_IDX, F32), (row, D_ATTN, F32), (row, D_ATTN, F32)]
    out_specs = [kind(w) for kind, w, _ in outs]
    out_shape = [(shp if kind is row else tshp)(w, dt) for kind, w, dt in outs]
    return pl.pallas_call(
        functools.partial(_inproj_kernel, prompt=prompt),
        grid=(n // tm,),
        in_specs=[row(d), ms(3), ms(4), pl.BlockSpec((None, D_IN_PAD, d), lambda i: (layer, 0, 0))],
        out_specs=out_specs,
        out_shape=out_shape,
        compiler_params=_cparams(("arbitrary",)),
        name="inproj",
    )(x, mod, mod, w_in_t)


def _score_to_key(score, keep):
    bits = pltpu.bitcast(jnp.where(score == 0.0, 0.0, score), I32)
    key = jnp.where(bits < 0, bits ^ INT_MAX, bits)
    return jnp.where(keep, key, INT_MIN)


I16 = jnp.int16
I16_MIN = -2 ** 15


def _key_chunk(ks, tk, axis):
    return (slice(None), pl.ds(ks, tk)) if axis == 1 else (pl.ds(ks, tk), slice(None))


def _fold_tiles(x, axis, tile, op):
    parts = [lax.slice_in_dim(x, i, i + tile, axis=axis) for i in range(0, x.shape[axis], tile)]
    while len(parts) > 1:
        parts = [op(a, b) for a, b in zip(parts[0::2], parts[1::2])] + parts[len(parts) & ~1:]
    return parts[0]


def _acc_shape(n_problems, axis, dtype):
    sublane_tile = 32 // jnp.dtype(dtype).itemsize
    return (n_problems, LANES) if axis == 1 else (sublane_tile, n_problems)


def _count16(k16_ref, cand, nchunks, tk, axis):
    shape = _acc_shape(k16_ref.shape[1 - axis], axis, I16)
    cand_b = jnp.broadcast_to(cand, shape).astype(I16)
    one, zero = jnp.int16(1), jnp.int16(0)

    def body(c, acc):
        kk = k16_ref[_key_chunk(pl.multiple_of(c * tk, tk), tk, axis)]
        hits = [jnp.where(lax.slice_in_dim(kk, i, i + shape[axis], axis=axis) >= cand_b, one, zero)
                for i in range(0, tk, shape[axis])]
        while len(hits) > 1:
            hits = [a + b for a, b in zip(hits[0::2], hits[1::2])] + hits[len(hits) & ~1:]
        return acc + hits[0]

    acc = lax.fori_loop(0, nchunks, body, jnp.zeros(shape, I16))
    return jnp.sum(acc.astype(I32), axis=axis, keepdims=True)


def _search16(k16_ref, nchunks, tk, target, count_all, axis):
    def bit_body(i, carry):
        y, c_ge, c_gt = carry
        cand = y + lax.shift_left(jnp.int32(1), 15 - i)
        cnt = _count16(k16_ref, cand, nchunks, tk, axis)
        ok = cnt >= target
        return jnp.where(ok, cand, y), jnp.where(ok, cnt, c_ge), jnp.where(ok, c_gt, cnt)

    return lax.fori_loop(0, 16, bit_body,
                         (jnp.full(count_all.shape, I16_MIN, I32), count_all, jnp.zeros(count_all.shape, I32)))


def _select_threshold(key_ref, k16_ref, j_ref, nchunks, tk, topk, index_bits, axis):
    n_problems = key_ref.shape[1 - axis]
    vec = (n_problems, 1) if axis == 1 else (1, n_problems)
    chunk_shape = (n_problems, tk) if axis == 1 else (tk, n_problems)

    def for_chunks(fn):
        def body(c, carry):
            idx = _key_chunk(pl.multiple_of(c * tk, tk), tk, axis)
            fn(idx, key_ref[idx])
            return carry
        lax.fori_loop(0, nchunks, body, 0)

    def store_high(idx, kk):
        k16_ref[idx] = lax.shift_right_arithmetic(kk, 16).astype(I16)

    for_chunks(store_high)
    hi, c_ge_hi, n_above = _search16(k16_ref, nchunks, tk, topk, jnp.full(vec, INT_MAX, I32), axis)

    def store_low(idx, kk):
        in_bucket = lax.shift_right_arithmetic(kk, 16) == hi
        low = (kk & 0xFFFF) + I16_MIN
        k16_ref[idx] = jnp.where(in_bucket, low, I16_MIN).astype(I16)

    for_chunks(store_low)
    lo, c_ge_lo, _ = _search16(k16_ref, nchunks, tk, topk - n_above, c_ge_hi - n_above, axis)
    x = lax.shift_left(hi, 16) + (lo - I16_MIN)
    cnt_x = n_above + c_ge_lo

    acc_shape = _acc_shape(n_problems, axis, I32)

    def count(pred):
        def body(c, cnt):
            ks = pl.multiple_of(c * tk, tk)
            m = pred(key_ref[_key_chunk(ks, tk, axis)], ks).astype(I32)
            return cnt + _fold_tiles(m, axis, acc_shape[axis], jnp.add)
        cnt = lax.fori_loop(0, nchunks, body, jnp.zeros(acc_shape, I32))
        return jnp.sum(cnt, axis=axis, keepdims=True)

    excess = (cnt_x > topk) & (x > INT_MIN)
    j_ref[...] = jnp.full(j_ref.shape, INT_MAX, I32)

    @pl.when(jnp.max(excess.astype(I32)) > 0)
    def _():
        need = topk - count(lambda kk, ks: kk > x)
        pos = lax.broadcasted_iota(I32, chunk_shape, axis)

        def idx_body(i, j):
            cand = j + lax.shift_left(jnp.int32(1), index_bits - 1 - i)
            cnt = count(lambda kk, ks: (kk == x) & (pos + ks < cand))
            return jnp.where(cnt < need, cand, j)

        j = lax.fori_loop(0, index_bits, idx_body, jnp.zeros(vec, I32))
        j_ref[...] = jnp.broadcast_to(jnp.where(excess, j, INT_MAX), j_ref.shape)

    return jnp.maximum(x, INT_MIN + 1), lax.slice_in_dim(j_ref[...], 0, 1, axis=axis)


def _selection_bias(kk, col, x, jmax):
    sel = (kk > x) | ((kk == x) & (col <= jmax))
    return jnp.where(sel, 0.0, MASK_BIAS)


def _attn_kernel(qit_ref, wit_ref, kiwib_ref, qt_ref, kb_ref, vt_ref, o_ref, key_ref, k16_ref, j_ref, s_ref,
                 *, tq, tk, topk, index_bits):
    t0 = pl.program_id(1) * tq
    nkc = (t0 + tq + tk - 1) // tk
    wit = wit_ref[...]
    heads = [slice(h * HEAD_DIM, (h + 1) * HEAD_DIM) for h in range(N_HEADS)]
    kpos = lax.broadcasted_iota(I32, (tk, tq), 0)
    qpos = lax.broadcasted_iota(I32, (tk, tq), 1) + t0

    def score_body(c, carry):
        ks = pl.multiple_of(c * tk, tk)
        ki = kiwib_ref[pl.ds(ks, tk), 0:D_IDX]
        acc = jnp.zeros((tk, tq), F32)
        for h in range(N_IDX_HEADS):
            s = _dot(ki, qit_ref[h * D_IDX:(h + 1) * D_IDX, :])
            acc = acc + jnp.maximum(s, 0.0) * wit[h:h + 1, :]
        key_ref[pl.ds(ks, tk), :] = _score_to_key(acc, kpos + ks <= qpos)
        return carry

    lax.fori_loop(0, nkc, score_body, 0)
    x, jmax = _select_threshold(key_ref, k16_ref, j_ref, nkc, tk, topk, index_bits, axis=0)

    tmax = (t0 + tq - 1).astype(F32)
    kposf = kpos.astype(F32)

    def attend_body(c, carry):
        ms, ls, accs = carry
        ks = pl.multiple_of(c * tk, tk)
        bias_c = _selection_bias(key_ref[pl.ds(ks, tk), :], kpos + ks, x, jmax)
        rel = kposf + (ks.astype(F32) - tmax)
        new_m = []
        for h in range(N_HEADS):
            s = (_dot(kb_ref[pl.ds(ks, tk), heads[h]], qt_ref[heads[h], :]) + bias_c
                 + ALIBI_SLOPES_LOG2[h] * rel)
            s_ref[h] = s
            new_m.append(jnp.maximum(ms[h], jnp.max(s, axis=0, keepdims=True)))
        new_l, new_a = [], []
        for h in range(N_HEADS):
            a = jnp.exp2(ms[h] - new_m[h])
            p = jnp.exp2(s_ref[h] - new_m[h])
            new_l.append(a * ls[h] + jnp.sum(p, axis=0, keepdims=True))
            new_a.append(a * accs[h] + _dot(vt_ref[heads[h], pl.ds(ks, tk)], p.astype(BF16)))
        return tuple(new_m), tuple(new_l), tuple(new_a)

    init = (tuple(jnp.full((1, tq), MASK_BIAS, F32) for _ in range(N_HEADS)),
            tuple(jnp.zeros((1, tq), F32) for _ in range(N_HEADS)),
            tuple(jnp.zeros((HEAD_DIM, tq), F32) for _ in range(N_HEADS)))
    _, ls, accs = lax.fori_loop(0, nkc, attend_body, init)
    out_t = jnp.concatenate([accs[h] / ls[h] for h in range(N_HEADS)], axis=0)
    o_ref[...] = out_t.T.astype(o_ref.dtype)


def _attn_prompt(qit, wit, kiwib, qt, kb, vtb, *, tq, tk):
    b, s, _ = kb.shape
    topk = min(TOPK_MAX, s // 4)
    blk_t = lambda w: pl.BlockSpec((None, w, tq), lambda bi, i: (bi, 0, i))
    full = lambda w: pl.BlockSpec((None, s, w), lambda bi, i: (bi, 0, 0))
    return pl.pallas_call(
        functools.partial(_attn_kernel, tq=tq, tk=tk, topk=topk, index_bits=max(s - 1, 1).bit_length()),
        grid=(b, s // tq),
        in_specs=[blk_t(D_QI), blk_t(N_IDX_HEADS), full(LANES), blk_t(D_ATTN), full(D_ATTN),
                  pl.BlockSpec((None, D_ATTN, s), lambda bi, i: (bi, 0, 0))],
        out_specs=pl.BlockSpec((None, tq, D_ATTN), lambda bi, i: (bi, i, 0)),
        out_shape=jax.ShapeDtypeStruct((b, s, D_ATTN), BF16),
        scratch_shapes=[pltpu.VMEM((s, tq), I32), pltpu.VMEM((s, tq), I16),
                        pltpu.VMEM(_acc_shape(tq, 0, I32), I32), pltpu.VMEM((N_HEADS, tk, tq), F32)],
        compiler_params=_cparams(("arbitrary", "arbitrary")),
        name="attn_prompt",
    )(qit, wit, kiwib, qt, kb, vtb)


def _pool_window_per_lane():
    lane = lax.broadcasted_iota(I32, (1, D_POOL), 1)
    return lax.shift_left(jnp.int32(POOL_WINDOWS[0]), lane // POOL_GW)


def _mix_tail(x, yconv_pre, d_pool, yattn, gcln_ref, bcln_ref, wpool_ref, spool_ref, wout_ref, g_ref, lng_ref,
              lnb_ref, alpha):
    yconv = _silu(_layer_norm(yconv_pre, gcln_ref[...], bcln_ref[...]))
    ypool = _dot(d_pool.astype(BF16), wpool_ref[...]) * spool_ref[...]
    cat = jnp.concatenate([yconv.astype(BF16), ypool.astype(BF16), yattn], axis=1)
    y = _dot(cat, wout_ref[...])
    return _layer_norm(alpha * x + g_ref[...] * y, lng_ref[...], lnb_ref[...])


SUBLANES = 8


def _shifted_rows(ext_ref, shift_ref, tt):
    span = tt + HALO - SUBLANES
    for r in range(1, SUBLANES):
        shift_ref[r, 0:span, :] = ext_ref[pl.ds(r, span), :]

    def rows(o):
        r = o % SUBLANES
        src = ext_ref if r == 0 else shift_ref.at[r]
        return src[pl.ds(o - r, tt), :]
    return rows


def _mixout_kernel(x_ref, vglu_ref, upool_ref, yattn_ref, wdw_ref, bdw_ref, gcln_ref, bcln_ref, wpool_ref,
                   spool_ref, wout_ref, g_ref, lng_ref, lnb_ref, o_ref, extc_ref, extp_ref, shift_ref,
                   *, tt, alpha):
    ti = pl.program_id(1)
    t0 = pl.multiple_of(ti * tt, tt)
    hs = pl.multiple_of(jnp.maximum(t0 - HALO, 0), 8)
    first = ti == 0
    extc_ref[0:HALO, :] = jnp.where(first, 0.0, vglu_ref[pl.ds(hs, HALO), :])
    extc_ref[HALO:, :] = vglu_ref[pl.ds(t0, tt), :]
    extp_ref[0:HALO, :] = jnp.where(first, 0.0, upool_ref[pl.ds(hs, HALO), :])
    u = upool_ref[pl.ds(t0, tt), :]
    extp_ref[HALO:, :] = u

    rows = _shifted_rows(extc_ref, shift_ref, tt)
    y = jnp.broadcast_to(bdw_ref[...], (tt, D_CONV))
    for j in range(CONV_W):
        y = y + rows(HALO - CONV_STATE + j) * wdw_ref[j:j + 1, :]

    rows = _shifted_rows(extp_ref, shift_ref, tt)
    wl = _pool_window_per_lane()
    win = u
    for i in range(1, max(POOL_WINDOWS)):
        win = win + rows(HALO - i) * (i < wl).astype(F32)
    pos = lax.broadcasted_iota(I32, (tt, 1), 0) + t0
    cnt = jnp.minimum(pos + 1, wl).astype(F32)
    d = win / cnt - u

    o_ref[...] = _mix_tail(x_ref[...], y, d, yattn_ref[...], gcln_ref, bcln_ref, wpool_ref, spool_ref, wout_ref,
                           g_ref, lng_ref, lnb_ref, alpha)


def _mixout_prompt(x, vglu, upool, yattn, mod, wdw, bdw, gcln, bcln, wpool, spool, wout_b, layer, lng, lnb,
                   *, tt, alpha):
    b, s, d = x.shape
    blk = lambda w: pl.BlockSpec((None, tt, w), lambda bi, i: (bi, i, 0))
    full = lambda w: pl.BlockSpec((None, s, w), lambda bi, i: (bi, 0, 0))
    const = lambda a: pl.BlockSpec(a.shape, lambda bi, i: (0,) * a.ndim)
    return pl.pallas_call(
        functools.partial(_mixout_kernel, tt=tt, alpha=alpha),
        grid=(b, s // tt),
        in_specs=[blk(d), full(D_CONV), full(D_POOL), blk(D_ATTN), const(wdw), const(bdw), const(gcln),
                  const(bcln), const(wpool), const(spool),
                  pl.BlockSpec((None,) + wout_b.shape[1:], lambda bi, i: (layer, 0, 0)),
                  pl.BlockSpec((None, 1, d), lambda bi, i: (bi, 0, 5)), const(lng), const(lnb)],
        out_specs=blk(d),
        out_shape=jax.ShapeDtypeStruct((b, s, d), F32),
        scratch_shapes=[pltpu.VMEM((HALO + tt, D_CONV), F32), pltpu.VMEM((HALO + tt, D_POOL), F32),
                        pltpu.VMEM((SUBLANES, HALO + tt, D_CONV), F32)],
        compiler_params=_cparams(("arbitrary", "arbitrary")),
        name="mixout_prompt",
    )(x, vglu, upool, yattn, wdw, bdw, gcln, bcln, wpool, spool, wout_b, mod, lng, lnb)


def _sample_score_kernel(pt_ref, qi_ref, wi_ref, kinew_ref, *rest, n_pages):
    page_refs, o_ref = rest[:n_pages], rest[n_pages]
    qi = qi_ref[...]
    wi = wi_ref[...] * IDX_SCALE
    for p in range(n_pages):
        s = _dot(qi, page_refs[p][...].astype(BF16))
        o_ref[p:p + 1, :] = jnp.sum(jnp.maximum(s, 0.0) * wi, axis=0, keepdims=True)
    kn = kinew_ref[...].astype(BF16).astype(F32)
    sn = jnp.sum(qi.astype(F32) * kn, axis=1, keepdims=True)
    snew = jnp.sum(jnp.maximum(sn, 0.0) * wi, axis=0, keepdims=True)
    lane = lax.broadcasted_iota(I32, (1, PAGE_SIZE), 1)
    o_ref[n_pages:n_pages + 1, :] = jnp.where(lane == 0, snew, -jnp.inf)


def _sample_scores(page_table, qi3, wi3, kinew3, kidx_t, layer):
    bd, n_pages = page_table.shape
    page_spec = lambda p: pl.BlockSpec((None, None, D_IDX, PAGE_SIZE), lambda b, pt: (layer, pt[b, p], 0, 0))
    return pl.pallas_call(
        functools.partial(_sample_score_kernel, n_pages=n_pages),
        grid_spec=pltpu.PrefetchScalarGridSpec(
            num_scalar_prefetch=1,
            grid=(bd,),
            in_specs=[pl.BlockSpec((None, N_IDX_HEADS, D_IDX), lambda b, pt: (b, 0, 0)),
                      pl.BlockSpec((None, N_IDX_HEADS, 1), lambda b, pt: (b, 0, 0)),
                      pl.BlockSpec((None, 1, D_IDX), lambda b, pt: (b, 0, 0))]
            + [page_spec(p) for p in range(n_pages)],
            out_specs=pl.BlockSpec((None, n_pages + 1, PAGE_SIZE), lambda b, pt: (b, 0, 0)),
        ),
        out_shape=jax.ShapeDtypeStruct((bd, n_pages + 1, PAGE_SIZE), F32),
        compiler_params=_cparams(("arbitrary",)),
        name="sample_scores",
    )(page_table, qi3, wi3, kinew3, *([kidx_t] * n_pages))


def _sample_select_kernel(s_ref, bias_ref, key_ref, k16_ref, j_ref, *, topk, index_bits):
    s = s_ref[...]
    key_ref[...] = _score_to_key(s, s > -jnp.inf)
    x, jmax = _select_threshold(key_ref, k16_ref, j_ref, s.shape[1] // LANES, LANES, topk, index_bits, axis=1)
    col = lax.broadcasted_iota(I32, s.shape, 1)
    bias_ref[...] = _selection_bias(key_ref[...], col, x, jmax)


def _sample_select(scores2d, topk):
    bd, n = scores2d.shape
    return pl.pallas_call(
        functools.partial(_sample_select_kernel, topk=topk, index_bits=max(n - 1, 1).bit_length()),
        out_shape=jax.ShapeDtypeStruct((bd, n), F32),
        scratch_shapes=[pltpu.VMEM((bd, n), I32), pltpu.VMEM((bd, n), I16), pltpu.VMEM((bd, LANES), I32)],
        compiler_params=pltpu.CompilerParams(vmem_limit_bytes=VMEM_LIMIT),
        name="sample_select",
    )(scores2d)


def _sample_attn_kernel(pt_ref, q_ref, knew_ref, vnew_ref, bias_ref, *rest, n_pages):
    kt_refs, vt_refs, o_ref = rest[:n_pages], rest[n_pages:2 * n_pages], rest[2 * n_pages]
    past = n_pages * PAGE_SIZE

    hrow = lax.broadcasted_iota(I32, (N_HEADS, D_ATTN), 0)
    hcol = lax.broadcasted_iota(I32, (N_HEADS, D_ATTN), 1) // HEAD_DIM
    own = hrow == hcol
    qbd = jnp.where(own, jnp.broadcast_to(q_ref[...], (N_HEADS, D_ATTN)), 0.0)
    qbd_b = qbd.astype(BF16)
    slopes = LOG2E * jnp.exp2(-8.0 * (lax.broadcasted_iota(I32, (N_HEADS, 1), 0) + 1).astype(F32) / N_HEADS)

    s = jnp.concatenate([_dot(qbd_b, kt_refs[i][...].astype(BF16)) for i in range(n_pages)], axis=1)
    dist = (past - lax.broadcasted_iota(I32, (1, past), 1)).astype(F32)
    s = s - slopes * dist + bias_ref[:, 0:past]
    sn = jnp.sum(qbd * knew_ref[...], axis=1, keepdims=True) + bias_ref[:, past:past + 1]

    m = jnp.maximum(jnp.max(s, axis=1, keepdims=True), sn)
    p = jnp.exp2(s - m)
    pn = jnp.exp2(sn - m)
    l = jnp.sum(p, axis=1, keepdims=True) + pn
    pb = p.astype(BF16)
    acc = pn * vnew_ref[...]
    for i in range(n_pages):
        acc = acc + _dot_nt(pb[:, i * PAGE_SIZE:(i + 1) * PAGE_SIZE], vt_refs[i][...].astype(BF16))
    out = jnp.where(own, acc / l, 0.0)
    o_ref[...] = jnp.sum(out, axis=0, keepdims=True).astype(o_ref.dtype)


def _sample_attn(page_table, q3, knew3, vnew3, bias3, ck_t, cv_t, layer):
    bd, n_pages = page_table.shape
    page_spec = lambda i: pl.BlockSpec((None, None, D_ATTN, PAGE_SIZE), lambda b, pt: (layer, pt[b, i], 0, 0))
    vec = pl.BlockSpec((None, 1, D_ATTN), lambda b, pt: (b, 0, 0))
    return pl.pallas_call(
        functools.partial(_sample_attn_kernel, n_pages=n_pages),
        grid_spec=pltpu.PrefetchScalarGridSpec(
            num_scalar_prefetch=1,
            grid=(bd,),
            in_specs=[vec, vec, vec, pl.BlockSpec((None, 1, bias3.shape[2]), lambda b, pt: (b, 0, 0))]
            + [page_spec(i) for i in range(n_pages)] * 2,
            out_specs=vec,
        ),
        out_shape=jax.ShapeDtypeStruct((bd, 1, D_ATTN), BF16),
        compiler_params=_cparams(("arbitrary",)),
        name="sample_attn",
    )(page_table, q3, knew3, vnew3, bias3, *([ck_t] * n_pages), *([cv_t] * n_pages))


def _mixout_sample_kernel(x_ref, vnew_ref, unew_ref, yattn_ref, stc_ref, stp_ref, wdw_ref, bdw_ref, gcln_ref,
                          bcln_ref, wpool_ref, spool_ref, wout_ref, g_ref, lng_ref, lnb_ref, o_ref,
                          *, alpha, pos0):
    unew = unew_ref[...]
    y = bdw_ref[...] + vnew_ref[...] * wdw_ref[CONV_STATE:CONV_STATE + 1, :]
    for j in range(CONV_STATE):
        y = y + stc_ref[j] * wdw_ref[j:j + 1, :]

    wl = _pool_window_per_lane()
    win = unew
    for j in range(POOL_STATE):
        win = win + stp_ref[j] * ((POOL_STATE - j) < wl).astype(F32)
    cnt = jnp.minimum(pos0 + 1, wl).astype(F32)
    d = win / cnt - unew

    o_ref[...] = _mix_tail(x_ref[...], y, d, yattn_ref[...], gcln_ref, bcln_ref, wpool_ref, spool_ref, wout_ref,
                           g_ref, lng_ref, lnb_ref, alpha)


def _mixout_sample(x, vnew, unew, yattn, stc_t, stp_t, layer, mod, wdw, bdw, gcln, bcln, wpool, spool, wout_b, lng,
                   lnb, *, alpha, pos0):
    bd, d = x.shape
    const = lambda a: pl.BlockSpec(a.shape, lambda i: (0,) * a.ndim)
    st = lambda a: pl.BlockSpec((None,) + a.shape[1:], lambda i: (layer, 0, 0, 0))
    return pl.pallas_call(
        functools.partial(_mixout_sample_kernel, alpha=alpha, pos0=pos0),
        grid=(1,),
        in_specs=[const(x), const(vnew), const(unew), const(yattn), st(stc_t), st(stp_t), const(wdw),
                  const(bdw), const(gcln), const(bcln), const(wpool), const(spool),
                  pl.BlockSpec((None,) + wout_b.shape[1:], lambda i: (layer, 0, 0)),
                  pl.BlockSpec((None, bd, d), lambda i: (0, 0, 5)), const(lng), const(lnb)],
        out_specs=const(x),
        out_shape=jax.ShapeDtypeStruct((bd, d), F32),
        compiler_params=_cparams(("arbitrary",)),
        name="mixout_sample",
    )(x, vnew, unew, yattn, stc_t, stp_t, wdw, bdw, gcln, bcln, wpool, spool, wout_b, mod, lng, lnb)


def _block_diag(w):
    g, a, b = w.shape
    out = jnp.zeros((g * a, g * b), w.dtype)
    for i in range(g):
        out = out.at[i * a:(i + 1) * a, i * b:(i + 1) * b].set(w[i])
    return out


def _tile(n, pref):
    t = min(n, pref)
    while n % t:
        t //= 2
    return t


def kernel(x_prompt, x_sample, cache_k, cache_v, cache_kidx, state_conv, state_pool, page_table, c_prompt, c_sample,
           w_ada, b_ada, ln_g, ln_b, w_ff_in, w_ff_out, w_in, w_dw, b_dw, g_conv_ln, b_conv_ln, w_pool, s_pool,
           w_out):
    bp, s, d = x_prompt.shape
    bd, t_dec, _ = x_sample.shape
    assert t_dec == 1, "the sample path handles one decode position per sequence"
    depth = w_ada.shape[0]
    n_phys = cache_k.shape[1]
    n_pages = page_table.shape[1]
    past = n_pages * PAGE_SIZE
    alpha = (2 * depth) ** 0.25
    topk_s = min(TOPK_MAX, (past + t_dec) // 4)

    r = bp + bd
    r_pad = -(-r // 8) * 8
    c_all = jnp.concatenate([c_prompt, c_sample, jnp.zeros((r_pad - r, d), F32)], axis=0)
    mod_all = _ada(c_all, w_ada, b_ada)

    wff_in = w_ff_in.astype(BF16)
    wff_out = w_ff_out.astype(BF16)
    w_in_t = jnp.pad(jnp.transpose(w_in, (0, 2, 1)), ((0, 0), (0, D_IN_PAD - D_IN), (0, 0))).astype(BF16)
    wout_b = w_out.astype(BF16)
    wdw_pad = jnp.pad(w_dw, ((0, 0), (0, HALO - CONV_W), (0, 0)))
    ck_t = jnp.transpose(cache_k, (0, 1, 3, 4, 2)).reshape(depth, n_phys, D_ATTN, PAGE_SIZE)
    cv_t = jnp.transpose(cache_v, (0, 1, 3, 4, 2)).reshape(depth, n_phys, D_ATTN, PAGE_SIZE)
    kidx_t = jnp.transpose(cache_kidx, (0, 1, 3, 2))
    stc_t = jnp.transpose(state_conv, (0, 2, 1, 3))
    stp_t = jnp.transpose(state_pool, (0, 2, 1, 3))

    tm = _tile(s, 512)
    tq = _tile(s, 256)
    tk = _tile(s, 512)

    xp = x_prompt.reshape(bp * s, d)
    xs = x_sample.reshape(bd, d)
    outs_p = [[] for _ in range(5)]
    outs_s = [[] for _ in range(5)]
    for l in range(depth):
        mod_p = mod_all[l, :bp].reshape(bp, 1, 9 * d)
        mod_s = mod_all[l, bp:r].reshape(1, bd, 9 * d)
        lng = [ln_g[l, i].reshape(1, d) for i in range(3)]
        lnb = [ln_b[l, i].reshape(1, d) for i in range(3)]
        mix_w = (wdw_pad[l], b_dw[l].reshape(1, D_CONV), g_conv_ln[l].reshape(1, D_CONV),
                 b_conv_ln[l].reshape(1, D_CONV), _block_diag(w_pool[l]).astype(BF16),
                 s_pool[l].reshape(1, D_POOL), wout_b, l)

        kw = dict(per_row=False, rows_per_seq=s, tm=tm)
        xp = _ffn(xp, mod_p, 0, wff_in, wff_out, l, 0, lng[0], lnb[0], alpha=alpha, n_sub=2, **kw)
        vglu, upool, kiwib, kb, qt, qit, wit, kt, vt, vtb, kit = _inproj(xp, mod_p, w_in_t, l, prompt=True, **kw)
        sq = lambda a: a.reshape(bp, s, a.shape[-1])
        yattn = _attn_prompt(qit, wit, sq(kiwib), qt, sq(kb), vtb, tq=tq, tk=tk)
        xp = _mixout_prompt(sq(xp), sq(vglu), sq(upool), yattn, mod_p, *mix_w, lng[1], lnb[1],
                            tt=tm, alpha=alpha).reshape(bp * s, d)
        xp = _ffn(xp, mod_p, 6, wff_in, wff_out, l, 1, lng[2], lnb[2], alpha=alpha, n_sub=2, **kw)
        outs_p[0].append(kt)
        outs_p[1].append(vt)
        outs_p[2].append(kit)
        outs_p[3].append(sq(vglu)[:, s - CONV_STATE:])
        outs_p[4].append(sq(upool)[:, s - POOL_STATE:])

        kw = dict(per_row=True, rows_per_seq=1, tm=bd)
        xs = _ffn(xs, mod_s, 0, wff_in, wff_out, l, 0, lng[0], lnb[0], alpha=alpha, n_sub=1, **kw)
        vglu, upool, q, qi, kiwi, kt, vt, kit, k_rm, v_rm = _inproj(xs, mod_s, w_in_t, l, prompt=False, **kw)
        scores = _sample_scores(page_table, qi.reshape(bd, N_IDX_HEADS, D_IDX),
                                kiwi[:, D_IDX:D_IDX + N_IDX_HEADS].reshape(bd, N_IDX_HEADS, 1),
                                kiwi[:, :D_IDX].reshape(bd, 1, D_IDX), kidx_t, l)
        bias = _sample_select(scores.reshape(bd, (n_pages + 1) * PAGE_SIZE), topk_s)
        yattn = _sample_attn(page_table, q.astype(F32).reshape(bd, 1, D_ATTN), k_rm.reshape(bd, 1, D_ATTN),
                             v_rm.reshape(bd, 1, D_ATTN), bias.reshape(bd, 1, (n_pages + 1) * PAGE_SIZE),
                             ck_t, cv_t, l).reshape(bd, D_ATTN)
        xs = _mixout_sample(xs, vglu, upool, yattn, stc_t, stp_t, l, mod_s, *mix_w[:-1], lng[1], lnb[1],
                            alpha=alpha, pos0=past)
        xs = _ffn(xs, mod_s, 6, wff_in, wff_out, l, 1, lng[2], lnb[2], alpha=alpha, n_sub=1, **kw)
        outs_s[0].append(kt[0])
        outs_s[1].append(vt[0])
        outs_s[2].append(kit[0])
        outs_s[3].append(vglu)
        outs_s[4].append(upool)

    k_p, v_p = [jnp.transpose(jnp.stack(o).reshape(depth, bp, N_HEADS, HEAD_DIM, s), (0, 1, 4, 2, 3))
                for o in outs_p[:2]]
    kidx_p = jnp.transpose(jnp.stack(outs_p[2]), (0, 1, 3, 2))
    k_s, v_s = [jnp.transpose(jnp.stack(o).reshape(depth, N_HEADS, HEAD_DIM, bd), (0, 3, 1, 2))
                .reshape(depth, bd, 1, N_HEADS, HEAD_DIM) for o in outs_s[:2]]
    kidx_s = jnp.transpose(jnp.stack(outs_s[2]), (0, 2, 1)).reshape(depth, bd, 1, D_IDX)
    conv_s = jnp.transpose(jnp.concatenate([stc_t[:, 1:], jnp.stack(outs_s[3])[:, None]], axis=1), (0, 2, 1, 3))
    pool_s = jnp.transpose(jnp.concatenate([stp_t[:, 1:], jnp.stack(outs_s[4])[:, None]], axis=1), (0, 2, 1, 3))
    return (xp.reshape(bp, s, d), xs.reshape(bd, 1, d), k_p, v_p, kidx_p, jnp.stack(outs_p[3]),
            jnp.stack(outs_p[4]), k_s, v_s, kidx_s, conv_s, pool_s)
```
